```python
import math
import jax
import jax.numpy as jnp
from jax import lax
import numpy as np

D_MODEL = 1024
BATCH = 8
SEQ = 4096
DEPTH = 4

N_HEADS = 8
HEAD_DIM = 64
ATTN_WIDTH = N_HEADS * HEAD_DIM
CONV_WIDTH = 512
CONV_KERNEL = 31
MOBA_BLOCK = 256
MOBA_TOPK = 3
Q_CHUNK = 32
REL_BUCKETS = 32
REL_MAX_DIST = 128
D_FF = 3584
N_EXPERTS = 8
TOP_K = 2
PLE_DIM = 256
LN_EPS = 1e-5
DEEPNORM_ALPHA = (2 * DEPTH) ** 0.25
DEEPNORM_BETA = (8 * DEPTH) ** -0.25

kernel_name = 'hybrid_conv_moba_moe_deepnorm'


def layer_norm(x, g, b):
    xf = x.astype(jnp.float32)
    mu = jnp.mean(xf, axis=-1, keepdims=True)
    var = jnp.mean(jnp.square(xf - mu), axis=-1, keepdims=True)
    y = (xf - mu) * lax.rsqrt(var + LN_EPS) * g.astype(jnp.float32) + b.astype(jnp.float32)
    return y.astype(x.dtype)


def rel_bucket(dist):
    n = jnp.maximum(dist, 0)
    max_exact = REL_BUCKETS // 2
    nf = jnp.maximum(n, 1).astype(jnp.float32)
    large = max_exact + (jnp.log(nf / max_exact) / math.log(REL_MAX_DIST / max_exact)
                         * (REL_BUCKETS - max_exact)).astype(jnp.int32)
    large = jnp.minimum(large, REL_BUCKETS - 1)
    return jnp.where(n < max_exact, n, large)


def conv_branch(a, w_dw, b_dw, g, b, w_proj):
    h = a[..., :CONV_WIDTH] * jax.nn.sigmoid(a[..., CONV_WIDTH:])
    h = lax.conv_general_dilated(h, w_dw[:, None, :], window_strides=(1,),
                                 padding=[(CONV_KERNEL - 1, 0)],
                                 dimension_numbers=('NWC', 'WIO', 'NWC'),
                                 feature_group_count=CONV_WIDTH) + b_dw
    h = jax.nn.silu(layer_norm(h, g, b))
    return h @ w_proj


def moba_attention(q, k, v, rel_bias):
    bsz, nh, s, dh = q.shape
    nb = -(-s // MOBA_BLOCK)
    pad = ((0, 0), (0, 0), (0, nb * MOBA_BLOCK - s), (0, 0))
    k_blk = jnp.pad(k, pad).reshape(bsz, nh, nb, MOBA_BLOCK, dh)
    v_blk = jnp.pad(v, pad).reshape(bsz, nh, nb, MOBA_BLOCK, dh)
    k_mean = jnp.mean(k_blk.astype(jnp.float32), axis=3)
    k_sel = min(MOBA_TOPK, nb - 1)
    scale = HEAD_DIM ** -0.5
    bias_t = rel_bias.T
    b_i = jnp.arange(bsz)[:, None, None, None]
    h_i = jnp.arange(nh)[None, :, None, None]
    h_i5 = jnp.arange(nh)[None, :, None, None, None]
    blk_ar = jnp.arange(MOBA_BLOCK)

    def chunk(c):
        q0 = c * Q_CHUNK
        qc = lax.dynamic_slice_in_dim(q, q0, Q_CHUNK, axis=2)
        qpos = q0 + jnp.arange(Q_CHUNK)
        own = q0 // MOBA_BLOCK
        k_own = lax.dynamic_index_in_dim(k_blk, own, axis=2, keepdims=False)
        v_own = lax.dynamic_index_in_dim(v_blk, own, axis=2, keepdims=False)
        dist_own = qpos[:, None] - (own * MOBA_BLOCK + blk_ar)[None, :]
        logit_own = (jnp.einsum('bhqd,bhkd->bhqk', qc, k_own).astype(jnp.float32) * scale
                     + bias_t[:, rel_bucket(dist_own)].astype(jnp.float32))
        logit_own = jnp.where(dist_own >= 0, logit_own, -jnp.inf)
        if k_sel == 0:
            probs = jax.nn.softmax(logit_own, axis=-1)
            return jnp.einsum('bhqk,bhkd->bhqd', probs.astype(v.dtype), v_own)
        score = jnp.einsum('bhqd,bhnd->bhqn', qc.astype(jnp.float32), k_mean)
        score = jnp.where(jnp.arange(nb) < own, score, -jnp.inf)
        _, idx = lax.top_k(score, k_sel)
        valid = jnp.arange(k_sel) < own
        k_g = k_blk[b_i, h_i, idx]
        v_g = v_blk[b_i, h_i, idx]
        kpos_sel = idx[..., None] * MOBA_BLOCK + blk_ar
        dist_sel = qpos[:, None, None] - kpos_sel
        logit_sel = (jnp.einsum('bhqd,bhqnkd->bhqnk', qc, k_g).astype(jnp.float32) * scale
                     + bias_t[h_i5, rel_bucket(dist_sel)].astype(jnp.float32))
        logit_sel = jnp.where(valid[:, None], logit_sel, -jnp.inf)
        logits = jnp.concatenate(
            [logit_own, logit_sel.reshape(bsz, nh, Q_CHUNK, k_sel * MOBA_BLOCK)], axis=-1)
        probs = jax.nn.softmax(logits, axis=-1).astype(v.dtype)
        p_own = probs[..., :MOBA_BLOCK]
        p_sel = probs[..., MOBA_BLOCK:].reshape(bsz, nh, Q_CHUNK, k_sel, MOBA_BLOCK)
        return (jnp.einsum('bhqk,bhkd->bhqd', p_own, v_own)
                + jnp.einsum('bhqnk,bhqnkd->bhqd', p_sel, v_g))

    outs = lax.map(chunk, jnp.arange(s // Q_CHUNK))
    return jnp.transpose(outs, (1, 2, 0, 3, 4)).reshape(bsz, nh, s, dh)


def swiglu(x, wg, wu, wd):
    return (jax.nn.silu(x @ wg) * (x @ wu)) @ wd


def moe_swiglu(x, router_w, wg, wu, wd):
    logits = (x @ router_w).astype(jnp.float32)
    top_v, top_i = lax.top_k(logits, TOP_K)
    top_p = jax.nn.softmax(top_v, axis=-1)
    gates = jnp.sum(jax.nn.one_hot(top_i, N_EXPERTS, dtype=jnp.float32) * top_p[..., None], axis=-2)
    out = jnp.zeros_like(x)
    for e in range(N_EXPERTS):
        out = out + gates[..., e:e + 1].astype(x.dtype) * swiglu(x, wg[e], wu[e], wd[e])
    return out


def _normal(key, shape, scale):
    return jax.random.normal(key, shape, jnp.float32) * scale


def setup_inputs(seed: int = 0) -> dict:
    key = jax.random.key(seed)
    k = jax.random.split(key, 28)
    L = DEPTH
    LD = (DEPTH + 1) // 2
    LM = DEPTH // 2
    D = D_MODEL
    C = CONV_WIDTH
    A = ATTN_WIDTH
    F = D_FF
    E = N_EXPERTS
    s_d = D ** -0.5
    beta = DEEPNORM_BETA
    x = _normal(k[0], (BATCH, SEQ, D), 1.0)
    p = _normal(k[1], (DEPTH, BATCH, SEQ, PLE_DIM), 1.0)
    rel_bias = _normal(k[2], (REL_BUCKETS, N_HEADS), 0.5)
    w_in = jnp.concatenate([
        _normal(k[3], (L, D, 2 * C), s_d),
        _normal(k[4], (L, D, 2 * A), s_d),
        _normal(k[5], (L, D, A), s_d * beta),
        _normal(k[6], (L, D, 2 * D), s_d),
    ], axis=-1)
    conv_w = _normal(k[7], (L, CONV_KERNEL, C), CONV_KERNEL ** -0.5)
    conv_b = _normal(k[8], (L, C), 0.02)
    conv_ln_g = 1.0 + _normal(k[9], (L, C), 0.02)
    conv_ln_b = _normal(k[10], (L, C), 0.02)
    w_conv_proj = _normal(k[11], (L, C, D), beta * C ** -0.5)
    w_attn_proj = _normal(k[12], (L, A, D), beta * A ** -0.5)
    w_out = _normal(k[13], (L, D, D), beta * s_d)
    ln1_g = 1.0 + _normal(k[14], (L, D), 0.02)
    ln1_b = _normal(k[15], (L, D), 0.02)
    w_ple_gate = _normal(k[16], (L, D, D), s_d)
    w_ple_proj = _normal(k[17], (L, PLE_DIM, D), beta * PLE_DIM ** -0.5)
    ln2_g = 1.0 + _normal(k[18], (L, D), 0.02)
    ln2_b = _normal(k[19], (L, D), 0.02)
    ffn_w_gate = _normal(k[20], (LD, D, F), s_d)
    ffn_w_up = _normal(k[21], (LD, D, F), s_d)
    ffn_w_down = _normal(k[22], (LD, F, D), beta * F ** -0.5)
    router_w = _normal(k[23], (LM, D, E), s_d)
    exp_w_gate = _normal(k[24], (LM, E, D, F), s_d)
    exp_w_up = _normal(k[25], (LM, E, D, F), s_d)
    exp_w_down = _normal(k[26], (LM, E, F, D), beta * F ** -0.5)
    return {'x': x, 'p': p, 'rel_bias': rel_bias, 'w_in': w_in, 'conv_w': conv_w,
            'conv_b': conv_b, 'conv_ln_g': conv_ln_g, 'conv_ln_b': conv_ln_b,
            'w_conv_proj': w_conv_proj, 'w_attn_proj': w_attn_proj, 'w_out': w_out,
            'ln1_g': ln1_g, 'ln1_b': ln1_b, 'w_ple_gate': w_ple_gate, 'w_ple_proj': w_ple_proj,
            'ln2_g': ln2_g, 'ln2_b': ln2_b, 'ffn_w_gate': ffn_w_gate, 'ffn_w_up': ffn_w_up,
            'ffn_w_down': ffn_w_down, 'router_w': router_w, 'exp_w_gate': exp_w_gate,
            'exp_w_up': exp_w_up, 'exp_w_down': exp_w_down}


def reference(x, p, rel_bias, w_in, conv_w, conv_b, conv_ln_g, conv_ln_b, w_conv_proj,
              w_attn_proj, w_out, ln1_g, ln1_b, w_ple_gate, w_ple_proj, ln2_g, ln2_b,
              ffn_w_gate, ffn_w_up, ffn_w_down, router_w, exp_w_gate, exp_w_up, exp_w_down):
    bsz, s, _ = x.shape
    C = CONV_WIDTH
    A = ATTN_WIDTH
    splits = [2 * C, 2 * C + A, 2 * C + 2 * A, 2 * C + 3 * A, 2 * C + 3 * A + D_MODEL]

    def heads(t):
        return jnp.transpose(t.reshape(bsz, s, N_HEADS, HEAD_DIM), (0, 2, 1, 3))

    for i in range(DEPTH):
        u = x @ w_in[i]
        a_conv, q, kk, vv, g_conv, g_attn = jnp.split(u, splits, axis=-1)
        y_conv = conv_branch(a_conv, conv_w[i], conv_b[i], conv_ln_g[i], conv_ln_b[i],
                             w_conv_proj[i])
        o = moba_attention(heads(q), heads(kk), heads(vv), rel_bias)
        o = jnp.transpose(o, (0, 2, 1, 3)).reshape(bsz, s, A)
        y_attn = o @ w_attn_proj[i]
        mixed = jax.nn.sigmoid(g_conv) * y_conv + jax.nn.sigmoid(g_attn) * y_attn
        x = layer_norm(DEEPNORM_ALPHA * x + mixed @ w_out[i], ln1_g[i], ln1_b[i])
        if i % 2 == 0:
            f = swiglu(x, ffn_w_gate[i // 2], ffn_w_up[i // 2], ffn_w_down[i // 2])
        else:
            f = moe_swiglu(x, router_w[i // 2], exp_w_gate[i // 2], exp_w_up[i // 2],
                           exp_w_down[i // 2])
        ple = jax.nn.sigmoid(x @ w_ple_gate[i]) * (p[i] @ w_ple_proj[i])
        x = layer_norm(DEEPNORM_ALPHA * x + f + ple, ln2_g[i], ln2_b[i])
    return x
```

```python
import functools
import math

import numpy as np
import jax
import jax.numpy as jnp
from jax import lax
from jax.experimental import pallas as pl
from jax.experimental.pallas import tpu as pltpu

N_HEADS = 8
HEAD_DIM = 64
CONV_KERNEL = 31
MOBA_BLOCK = 256
MOBA_TOPK = 3
REL_BUCKETS = 32
REL_MAX_DIST = 128
TOP_K = 2
LN_EPS = 1e-5

LANES = 128
BF16_SUBLANES = 16
HALO_ROWS = 32
VMEM_LIMIT = 48 * 1024 * 1024

F32 = jnp.float32
BF16 = jnp.bfloat16
NEG_INF = float("-inf")


def _sigmoid(t):
    return 1.0 / (1.0 + jnp.exp(-t))


def _layer_norm(t, g, b):
    mu = jnp.mean(t, axis=-1, keepdims=True)
    d = t - mu
    var = jnp.mean(d * d, axis=-1, keepdims=True)
    return d * lax.rsqrt(var + LN_EPS) * g + b


def _dot(a, b):
    return jnp.dot(a, b, preferred_element_type=F32)


def _rel_bucket(dist):
    n = jnp.maximum(dist, 0)
    max_exact = REL_BUCKETS // 2
    nf = jnp.maximum(n, 1).astype(F32)
    large = max_exact + (jnp.log(nf / max_exact) / math.log(REL_MAX_DIST / max_exact)
                         * (REL_BUCKETS - max_exact)).astype(jnp.int32)
    large = jnp.minimum(large, REL_BUCKETS - 1)
    return jnp.where(n < max_exact, n, large)


def _in_proj_kernel(x_ref, w_ref, h_ref, q_ref, k_ref, v_ref, sgc_ref, sga_ref, *, c, a, d):
    xb = x_ref[...].astype(BF16)
    o = 0
    glu_in = _dot(xb, w_ref[:, o:o + c])
    glu_gate = _dot(xb, w_ref[:, o + c:o + 2 * c])
    h_ref[...] = (glu_in * _sigmoid(glu_gate)).astype(BF16)
    o += 2 * c
    q_ref[...] = _dot(xb, w_ref[:, o:o + a]).astype(BF16)
    o += a
    k_ref[...] = _dot(xb, w_ref[:, o:o + a]).astype(BF16)
    o += a
    v_ref[...] = _dot(xb, w_ref[:, o:o + a]).astype(BF16)
    o += a
    sgc_ref[...] = _sigmoid(_dot(xb, w_ref[:, o:o + d])).astype(BF16)
    o += d
    sga_ref[...] = _sigmoid(_dot(xb, w_ref[:, o:o + d])).astype(BF16)


def _in_proj(x, w, *, c, a, tm):
    t, d = x.shape
    n = w.shape[1]
    row = lambda width: pl.BlockSpec((tm, width), lambda i: (i, 0))
    return pl.pallas_call(
        functools.partial(_in_proj_kernel, c=c, a=a, d=d),
        grid=(t // tm,),
        in_specs=[row(d), pl.BlockSpec((d, n), lambda i: (0, 0))],
        out_specs=[row(c), row(a), row(a), row(a), row(d), row(d)],
        out_shape=[jax.ShapeDtypeStruct((t, c), BF16)] + [jax.ShapeDtypeStruct((t, a), BF16)] * 3
        + [jax.ShapeDtypeStruct((t, d), BF16)] * 2,
        compiler_params=pltpu.CompilerParams(dimension_semantics=("arbitrary",),
                                             vmem_limit_bytes=VMEM_LIMIT),
        name="in_proj",
    )(x, w)


def _attn_kernel(cfar_ref, q_ref, k_ref, v_ref, bias_ref, o_ref, vt_scr, kmean_scr, sel_scr, *, nb, nbp):
    blk = MOBA_BLOCK
    pair = pl.program_id(1)
    i = pl.program_id(2)

    @pl.when(i == 0)
    def _():
        if nbp > nb:
            kmean_scr[...] = jnp.zeros_like(kmean_scr)
        for j in range(nb):
            kb = k_ref[j * blk:(j + 1) * blk, :].astype(F32)
            kmean_scr[j:j + 1, :] = jnp.sum(kb, axis=0, keepdims=True) * (1.0 / blk)
            vb = v_ref[j * blk:(j + 1) * blk, :].astype(F32)
            vt_scr[j] = vb.T.astype(BF16)

    qt = q_ref[...].astype(F32).T * (HEAD_DIM ** -0.5)
    qrow = lax.broadcasted_iota(jnp.int32, qt.shape, 0)
    kmean = kmean_scr[...].astype(BF16)
    brow = lax.broadcasted_iota(jnp.int32, (nbp, blk), 0)

    def softmax_step(carry, s, vt):
        m_old, l_old, acc_old = carry
        m_new = jnp.maximum(m_old, jnp.max(s, axis=0, keepdims=True))
        alpha = jnp.exp(m_old - m_new)
        p = jnp.exp(s - m_new)
        l_new = alpha * l_old + jnp.sum(p, axis=0, keepdims=True)
        acc_new = alpha * acc_old + _dot(vt, p.astype(BF16))
        return m_new, l_new, acc_new

    outs = []
    for hh in range(2):
        lo = hh * HEAD_DIM
        qh = jnp.where((qrow >= lo) & (qrow < lo + HEAD_DIM), qt, 0.0).astype(BF16)
        c_far = cfar_ref[2 * pair + hh]

        score = _dot(kmean, qh)
        cur = jnp.where(brow < i, score, NEG_INF)
        sel = jnp.zeros((nbp, blk), dtype=jnp.bool_)
        for t in range(MOBA_TOPK):
            mx = jnp.max(cur, axis=0, keepdims=True)
            first = jnp.min(jnp.where(cur == mx, brow, nbp), axis=0, keepdims=True)
            pick = brow == first
            sel = sel | (pick & (t < i))
            cur = jnp.where(pick, NEG_INF, cur)
        sel_scr[hh] = jnp.where(sel, 0.0, NEG_INF)

        def vt_block(j):
            return vt_scr[j, lo:lo + HEAD_DIM, :]

        def k_block(j):
            return k_ref[pl.ds(pl.multiple_of(j * blk, blk), blk), :]

        s = _dot(k_block(i), qh) + bias_ref[0, hh, 0]
        m0 = jnp.max(s, axis=0, keepdims=True)
        p0 = jnp.exp(s - m0)
        carry = (m0, jnp.sum(p0, axis=0, keepdims=True), _dot(vt_block(i), p0.astype(BF16)))

        def far_body(j, carry):
            s = _dot(k_block(j), qh) + (sel_scr[hh, pl.ds(j, 1), :] + c_far)
            return softmax_step(carry, s, vt_block(j))

        carry = lax.fori_loop(0, i - 1, far_body, carry)

        def prev_block(carry):
            s = _dot(k_block(i - 1), qh) + bias_ref[0, hh, 1] + sel_scr[hh, pl.ds(i - 1, 1), :]
            return softmax_step(carry, s, vt_block(i - 1))

        m, l, acc = lax.cond(i >= 1, prev_block, lambda c: c, carry)
        outs.append(acc / l)

    o_ref[...] = jnp.concatenate(outs, axis=0).T.astype(BF16)


def _attention(q, k, v, bias_tables, cfar, *, bsz, s):
    t, a = q.shape
    blk = MOBA_BLOCK
    nb = s // blk
    nbp = -(-nb // BF16_SUBLANES) * BF16_SUBLANES
    npairs = a // LANES
    return pl.pallas_call(
        functools.partial(_attn_kernel, nb=nb, nbp=nbp),
        grid=(bsz, npairs, nb),
        in_specs=[
            pl.BlockSpec(memory_space=pltpu.SMEM),
            pl.BlockSpec((blk, LANES), lambda b, p, i: (b * nb + i, p)),
            pl.BlockSpec((s, LANES), lambda b, p, i: (b, p)),
            pl.BlockSpec((s, LANES), lambda b, p, i: (b, p)),
            pl.BlockSpec((1, 2, 2, blk, blk), lambda b, p, i: (p, 0, 0, 0, 0)),
        ],
        out_specs=pl.BlockSpec((blk, LANES), lambda b, p, i: (b * nb + i, p)),
        out_shape=jax.ShapeDtypeStruct((t, a), BF16),
        scratch_shapes=[
            pltpu.VMEM((nb, LANES, blk), BF16),
            pltpu.VMEM((nbp, LANES), F32),
            pltpu.VMEM((2, nbp, blk), F32),
        ],
        compiler_params=pltpu.CompilerParams(dimension_semantics=("arbitrary", "arbitrary", "arbitrary"),
                                             vmem_limit_bytes=VMEM_LIMIT),
        name="moba_attention",
    )(cfar, q, k, v, bias_tables)


def _bias_tables(rel_bias):
    blk = MOBA_BLOCK
    n = np.float32(blk + 1)
    far_bucket = REL_BUCKETS // 2 + int(np.log(n / np.float32(REL_BUCKETS // 2)) / math.log(REL_MAX_DIST / (REL_BUCKETS // 2))
                                        * (REL_BUCKETS - REL_BUCKETS // 2))
    assert far_bucket >= REL_BUCKETS - 1
    bias_t = rel_bias.T.astype(F32)
    kk = jnp.arange(blk, dtype=jnp.int32)[:, None]
    qq = jnp.arange(blk, dtype=jnp.int32)[None, :]
    d_own = qq - kk
    own = jnp.where(d_own >= 0, bias_t[:, _rel_bucket(d_own)], NEG_INF)
    prev = bias_t[:, _rel_bucket(qq + blk - kk)]
    tables = jnp.stack([own, prev], axis=1)
    tables = tables.reshape(N_HEADS // 2, 2, 2, blk, blk)
    cfar = bias_t[:, REL_BUCKETS - 1]
    return tables, cfar


def _mix_kernel(x_ref, h_ref, halo_ref, o_ref, sgc_ref, sga_ref, cw_ref, cb_ref, cg_ref, cbeta_ref,
                wcp_ref, wap_ref, wout_ref, g1_ref, b1_ref, *rest, tm, tiles_per_seq, alpha, n_experts):
    if n_experts:
        rw_ref, x1_ref, gates_ref, hc_scr = rest
    else:
        x1_ref, hc_scr = rest
    i = pl.program_id(0)
    first = (i % tiles_per_seq) == 0
    hc_scr[0:HALO_ROWS, :] = jnp.where(first, 0.0, halo_ref[...].astype(F32))
    hc_scr[HALO_ROWS:, :] = h_ref[...].astype(F32)
    base = HALO_ROWS - (CONV_KERNEL - 1)
    acc = cb_ref[...] + cw_ref[0:1, :] * hc_scr[base:base + tm, :]
    for kk in range(1, CONV_KERNEL):
        acc = acc + cw_ref[kk:kk + 1, :] * hc_scr[base + kk:base + kk + tm, :]
    hn = _layer_norm(acc, cg_ref[...], cbeta_ref[...])
    hs = (hn * _sigmoid(hn)).astype(BF16)
    y_conv = _dot(hs, wcp_ref[...])
    y_attn = _dot(o_ref[...], wap_ref[...])
    mixed = sgc_ref[...].astype(F32) * y_conv + sga_ref[...].astype(F32) * y_attn
    z = alpha * x_ref[...] + _dot(mixed.astype(BF16), wout_ref[...])
    x1 = _layer_norm(z, g1_ref[...], b1_ref[...])
    x1_ref[...] = x1
    if n_experts:
        logits = _dot(x1.astype(BF16), rw_ref[...])
        lane = lax.broadcasted_iota(jnp.int32, logits.shape, 1)
        lg = jnp.where(lane < n_experts, logits, NEG_INF)
        v1 = jnp.max(lg, axis=-1, keepdims=True)
        i1 = jnp.min(jnp.where(lg == v1, lane, LANES), axis=-1, keepdims=True)
        lg2 = jnp.where(lane == i1, NEG_INF, lg)
        v2 = jnp.max(lg2, axis=-1, keepdims=True)
        i2 = jnp.min(jnp.where(lg2 == v2, lane, LANES), axis=-1, keepdims=True)
        e = jnp.exp(v2 - v1)
        gates_ref[...] = jnp.where(lane == i1, 1.0 / (1.0 + e), 0.0) + jnp.where(lane == i2, e / (1.0 + e), 0.0)


def _mix(x, h, o, sgc, sga, cw, cb, cg, cbeta, wcp, wap, wout, g1, b1, rw, *, s, tm, alpha, n_experts):
    t, d = x.shape
    c = h.shape[1]
    a = o.shape[1]
    row = lambda width: pl.BlockSpec((tm, width), lambda i: (i, 0))
    full = lambda arr: pl.BlockSpec(arr.shape, lambda i: (0,) * arr.ndim)
    halo_blocks = tm // HALO_ROWS
    in_specs = [row(d), row(c),
                pl.BlockSpec((HALO_ROWS, c), lambda i: (jnp.maximum(i * halo_blocks - 1, 0), 0)),
                row(a), row(d), row(d),
                full(cw), full(cb), full(cg), full(cbeta), full(wcp), full(wap), full(wout), full(g1), full(b1)]
    args = [x, h, h, o, sgc, sga, cw, cb, cg, cbeta, wcp, wap, wout, g1, b1]
    out_specs = [row(d)]
    out_shape = [jax.ShapeDtypeStruct((t, d), F32)]
    if n_experts:
        in_specs.append(full(rw))
        args.append(rw)
        out_specs.append(row(LANES))
        out_shape.append(jax.ShapeDtypeStruct((t, LANES), F32))
    res = pl.pallas_call(
        functools.partial(_mix_kernel, tm=tm, tiles_per_seq=s // tm, alpha=alpha, n_experts=n_experts),
        grid=(t // tm,),
        in_specs=in_specs,
        out_specs=out_specs,
        out_shape=out_shape,
        scratch_shapes=[pltpu.VMEM((tm + HALO_ROWS, c), F32)],
        compiler_params=pltpu.CompilerParams(dimension_semantics=("arbitrary",),
                                             vmem_limit_bytes=VMEM_LIMIT),
        name="mix",
    )(*args)
    return res if n_experts else (res[0], None)


def _ffn_kernel(x1_ref, p_ref, wg_ref, wu_ref, wd_ref, wpg_ref, wpp_ref, g2_ref, b2_ref, *rest, alpha, gated):
    if gated:
        gates_ref, out_ref, xb_scr, acc_scr = rest
    else:
        out_ref, xb_scr, acc_scr = rest
    e = pl.program_id(1)
    j = pl.program_id(2)

    @pl.when((e == 0) & (j == 0))
    def _():
        xb_scr[...] = x1_ref[...].astype(BF16)
        acc_scr[...] = jnp.zeros_like(acc_scr)

    xb = xb_scr[...]
    g = _dot(xb, wg_ref[0])
    u = _dot(xb, wu_ref[0])
    hm = g * _sigmoid(g) * u
    if gated:
        gates = gates_ref[...]
        lane = lax.broadcasted_iota(jnp.int32, gates.shape, 1)
        hm = hm * jnp.sum(jnp.where(lane == e, gates, 0.0), axis=-1, keepdims=True)
    acc_scr[...] += _dot(hm.astype(BF16), wd_ref[0])

    @pl.when((e == pl.num_programs(1) - 1) & (j == pl.num_programs(2) - 1))
    def _():
        ple = _sigmoid(_dot(xb, wpg_ref[...])) * _dot(p_ref[...].astype(BF16), wpp_ref[...])
        out_ref[...] = _layer_norm(alpha * x1_ref[...] + acc_scr[...] + ple, g2_ref[...], b2_ref[...])


def _ffn(x1, p, wg, wu, wd, wpg, wpp, g2, b2, gates, *, tm, fc, alpha):
    t, d = x1.shape
    ne, _, f = wg.shape
    gated = gates is not None
    row = lambda width: pl.BlockSpec((tm, width), lambda i, e, j: (i, 0))
    full = lambda arr: pl.BlockSpec(arr.shape, lambda i, e, j: (0,) * arr.ndim)
    in_specs = [row(d), row(p.shape[1]),
                pl.BlockSpec((1, d, fc), lambda i, e, j: (e, 0, j)),
                pl.BlockSpec((1, d, fc), lambda i, e, j: (e, 0, j)),
                pl.BlockSpec((1, fc, d), lambda i, e, j: (e, j, 0)),
                full(wpg), full(wpp), full(g2), full(b2)]
    args = [x1, p, wg, wu, wd, wpg, wpp, g2, b2]
    if gated:
        in_specs.append(row(LANES))
        args.append(gates)
    return pl.pallas_call(
        functools.partial(_ffn_kernel, alpha=alpha, gated=gated),
        grid=(t // tm, ne, f // fc),
        in_specs=in_specs,
        out_specs=row(d),
        out_shape=jax.ShapeDtypeStruct((t, d), F32),
        scratch_shapes=[pltpu.VMEM((tm, d), BF16), pltpu.VMEM((tm, d), F32)],
        compiler_params=pltpu.CompilerParams(dimension_semantics=("arbitrary", "arbitrary", "arbitrary"),
                                             vmem_limit_bytes=VMEM_LIMIT),
        name="ffn",
    )(*args)


def kernel(x, p, rel_bias, w_in, conv_w, conv_b, conv_ln_g, conv_ln_b, w_conv_proj, w_attn_proj, w_out, ln1_g, ln1_b, w_ple_gate, w_ple_proj, ln2_g, ln2_b, ffn_w_gate, ffn_w_up, ffn_w_down, router_w, exp_w_gate, exp_w_up, exp_w_down):
    bsz, s, d = x.shape
    depth = w_in.shape[0]
    c = conv_w.shape[2]
    a = w_attn_proj.shape[1]
    n_experts = router_w.shape[2]
    assert a == N_HEADS * HEAD_DIM and s % MOBA_BLOCK == 0 and conv_w.shape[1] == CONV_KERNEL
    alpha = (2 * depth) ** 0.25
    t = bsz * s
    tm = 512
    assert s % tm == 0

    tables, cfar = _bias_tables(rel_bias)
    xf = x.reshape(t, d)
    row2 = lambda v: v.reshape(1, -1).astype(F32)
    for i in range(depth):
        h, q, k, v, sgc, sga = _in_proj(xf, w_in[i].astype(BF16), c=c, a=a, tm=tm)
        o = _attention(q, k, v, tables, cfar, bsz=bsz, s=s)
        moe = i % 2 == 1
        cw = jnp.pad(conv_w[i], ((0, HALO_ROWS - CONV_KERNEL), (0, 0)))
        rw = jnp.pad(router_w[i // 2], ((0, 0), (0, LANES - n_experts))).astype(BF16) if moe else None
        x1, gates = _mix(xf, h, o, sgc, sga, cw, row2(conv_b[i]), row2(conv_ln_g[i]), row2(conv_ln_b[i]),
                         w_conv_proj[i].astype(BF16), w_attn_proj[i].astype(BF16), w_out[i].astype(BF16),
                         row2(ln1_g[i]), row2(ln1_b[i]), rw,
                         s=s, tm=tm, alpha=alpha, n_experts=n_experts if moe else 0)
        if moe:
            wg, wu, wd = exp_w_gate[i // 2], exp_w_up[i // 2], exp_w_down[i // 2]
        else:
            wg, wu, wd = ffn_w_gate[i // 2][None], ffn_w_up[i // 2][None], ffn_w_down[i // 2][None]
        xf = _ffn(x1, p[i].reshape(t, -1), wg.astype(BF16), wu.astype(BF16), wd.astype(BF16),
                  w_ple_gate[i].astype(BF16), w_ple_proj[i].astype(BF16), row2(ln2_g[i]), row2(ln2_b[i]), gates,
                  tm=tm, fc=512, alpha=alpha)
    return xf.reshape(bsz, s, d)
```

```python
import functools
import math

import numpy as np
import jax
import jax.numpy as jnp
from jax import lax
from jax.experimental import pallas as pl
from jax.experimental.pallas import tpu as pltpu

N_HEADS = 8
HEAD_DIM = 64
CONV_KERNEL = 31
MOBA_BLOCK = 256
MOBA_TOPK = 3
REL_BUCKETS = 32
REL_MAX_DIST = 128
TOP_K = 2
LN_EPS = 1e-5

LANES = 128
SUBLANES = 8
BF16_SUBLANES = 16
HALO_ROWS = 32
VMEM_LIMIT = 48 * 1024 * 1024

F32 = jnp.float32
BF16 = jnp.bfloat16
NEG_INF = float("-inf")


def _sigmoid(t):
    return 1.0 / (1.0 + jnp.exp(-t))


def _layer_norm(t, g, b):
    mu = jnp.mean(t, axis=-1, keepdims=True)
    d = t - mu
    var = jnp.mean(d * d, axis=-1, keepdims=True)
    return d * lax.rsqrt(var + LN_EPS) * g + b


def _dot(a, b):
    return jnp.dot(a, b, preferred_element_type=F32)


def _params(*semantics):
    return pltpu.CompilerParams(dimension_semantics=semantics, vmem_limit_bytes=VMEM_LIMIT)


def _rel_bucket(dist):
    n = jnp.maximum(dist, 0)
    max_exact = REL_BUCKETS // 2
    nf = jnp.maximum(n, 1).astype(F32)
    large = max_exact + (jnp.log(nf / max_exact) / math.log(REL_MAX_DIST / max_exact)
                         * (REL_BUCKETS - max_exact)).astype(jnp.int32)
    large = jnp.minimum(large, REL_BUCKETS - 1)
    return jnp.where(n < max_exact, n, large)


def _in_proj_kernel(x_ref, w_ref, h_ref, q_ref, k_ref, v_ref, sgc_ref, sga_ref, *, c, a, d):
    xb = x_ref[...].astype(BF16)
    o = 0
    glu_in = _dot(xb, w_ref[:, o:o + c])
    glu_gate = _dot(xb, w_ref[:, o + c:o + 2 * c])
    h_ref[...] = (glu_in * _sigmoid(glu_gate)).astype(BF16)
    o += 2 * c
    q_ref[...] = _dot(xb, w_ref[:, o:o + a]).astype(BF16)
    o += a
    k_ref[...] = _dot(xb, w_ref[:, o:o + a]).astype(BF16)
    o += a
    v_ref[...] = _dot(xb, w_ref[:, o:o + a]).astype(BF16)
    o += a
    sgc_ref[...] = _sigmoid(_dot(xb, w_ref[:, o:o + d])).astype(BF16)
    o += d
    sga_ref[...] = _sigmoid(_dot(xb, w_ref[:, o:o + d])).astype(BF16)


def _in_proj(x, w, *, c, a, tm):
    t, d = x.shape
    n = w.shape[1]
    row = lambda width: pl.BlockSpec((tm, width), lambda i: (i, 0))
    return pl.pallas_call(
        functools.partial(_in_proj_kernel, c=c, a=a, d=d),
        grid=(t // tm,),
        in_specs=[row(d), pl.BlockSpec((d, n), lambda i: (0, 0))],
        out_specs=[row(c), row(a), row(a), row(a), row(d), row(d)],
        out_shape=[jax.ShapeDtypeStruct((t, c), BF16)] + [jax.ShapeDtypeStruct((t, a), BF16)] * 3
        + [jax.ShapeDtypeStruct((t, d), BF16)] * 2,
        compiler_params=_params("arbitrary"),
        name="in_proj",
    )(x, w)


def _attn_kernel(cfar_ref, q_ref, k_ref, v_ref, bias_ref, o_ref, vt_scr, kmean_scr, sel_scr, *, nb, nbp):
    blk = MOBA_BLOCK
    pair = pl.program_id(1)
    i = pl.program_id(2)

    @pl.when(i == 0)
    def _():
        if nbp > nb:
            kmean_scr[...] = jnp.zeros_like(kmean_scr)
        for j in range(nb):
            kb = k_ref[j * blk:(j + 1) * blk, :].astype(F32)
            kmean_scr[j:j + 1, :] = jnp.sum(kb, axis=0, keepdims=True) * (1.0 / blk)
            vb = v_ref[j * blk:(j + 1) * blk, :].astype(F32)
            vt_scr[j] = vb.T.astype(BF16)

    qt = q_ref[...].astype(F32).T * (HEAD_DIM ** -0.5)
    qrow = lax.broadcasted_iota(jnp.int32, qt.shape, 0)
    kmean = kmean_scr[...].astype(BF16)
    brow = lax.broadcasted_iota(jnp.int32, (nbp, blk), 0)

    qhs, c_far = [], []
    for hh in range(2):
        lo = hh * HEAD_DIM
        qh = jnp.where((qrow >= lo) & (qrow < lo + HEAD_DIM), qt, 0.0).astype(BF16)
        qhs.append(qh)
        c_far.append(cfar_ref[2 * pair + hh])
        score = _dot(kmean, qh)
        cur = jnp.where(brow < i, score, NEG_INF)
        sel = jnp.zeros((nbp, blk), dtype=jnp.bool_)
        for t in range(MOBA_TOPK):
            mx = jnp.max(cur, axis=0, keepdims=True)
            first = jnp.min(jnp.where(cur == mx, brow, nbp), axis=0, keepdims=True)
            pick = brow == first
            sel = sel | (pick & (t < i))
            cur = jnp.where(pick, NEG_INF, cur)
        sel_scr[hh] = jnp.where(sel, 0.0, NEG_INF)

    def vt_block(j, hh):
        return vt_scr[j, hh * HEAD_DIM:(hh + 1) * HEAD_DIM, :]

    def k_block(j):
        return k_ref[pl.ds(pl.multiple_of(j * blk, blk), blk), :]

    def sel_row(j, hh):
        return sel_scr[hh, pl.ds(j, 1), :]

    jp = jnp.maximum(i - 1, 0)
    k_own, k_prev = k_block(i), k_block(jp)
    carry = []
    for hh in range(2):
        s_own = _dot(k_own, qhs[hh]) + bias_ref[0, hh, 0]
        s_prev = _dot(k_prev, qhs[hh]) + bias_ref[0, hh, 1] + sel_row(jp, hh)
        m = jnp.maximum(jnp.max(s_own, axis=0, keepdims=True), jnp.max(s_prev, axis=0, keepdims=True))
        p_own = jnp.exp(s_own - m)
        p_prev = jnp.exp(s_prev - m)
        l = jnp.sum(p_own, axis=0, keepdims=True) + jnp.sum(p_prev, axis=0, keepdims=True)
        acc = _dot(vt_block(i, hh), p_own.astype(BF16)) + _dot(vt_block(jp, hh), p_prev.astype(BF16))
        carry += [m, l, acc]

    n_far = i - 1

    def far_body(jj, carry):
        j0 = 2 * jj
        j1 = jnp.minimum(j0 + 1, jnp.maximum(n_far - 1, 0))
        has_j1 = j0 + 1 < n_far
        k0, k1 = k_block(j0), k_block(j1)
        out = []
        for hh in range(2):
            m_old, l_old, acc_old = carry[3 * hh:3 * hh + 3]
            s0 = _dot(k0, qhs[hh]) + (sel_row(j0, hh) + c_far[hh])
            s1 = _dot(k1, qhs[hh]) + jnp.where(has_j1, sel_row(j1, hh) + c_far[hh], NEG_INF)
            m_new = jnp.maximum(m_old, jnp.maximum(jnp.max(s0, axis=0, keepdims=True),
                                                   jnp.max(s1, axis=0, keepdims=True)))
            alpha = jnp.exp(m_old - m_new)
            p0 = jnp.exp(s0 - m_new)
            p1 = jnp.exp(s1 - m_new)
            l_new = alpha * l_old + jnp.sum(p0, axis=0, keepdims=True) + jnp.sum(p1, axis=0, keepdims=True)
            acc_new = (alpha * acc_old + _dot(vt_block(j0, hh), p0.astype(BF16))
                       + _dot(vt_block(j1, hh), p1.astype(BF16)))
            out += [m_new, l_new, acc_new]
        return tuple(out)

    carry = lax.fori_loop(0, (n_far + 1) // 2, far_body, tuple(carry))
    outs = [carry[3 * hh + 2] / carry[3 * hh + 1] for hh in range(2)]
    o_ref[...] = jnp.concatenate(outs, axis=0).T.astype(BF16)


def _attention(q, k, v, bias_tables, cfar, *, bsz, s):
    t, a = q.shape
    blk = MOBA_BLOCK
    nb = s // blk
    nbp = -(-nb // BF16_SUBLANES) * BF16_SUBLANES
    npairs = a // LANES
    return pl.pallas_call(
        functools.partial(_attn_kernel, nb=nb, nbp=nbp),
        grid=(bsz, npairs, nb),
        in_specs=[
            pl.BlockSpec(memory_space=pltpu.SMEM),
            pl.BlockSpec((blk, LANES), lambda b, p, i: (b * nb + i, p)),
            pl.BlockSpec((s, LANES), lambda b, p, i: (b, p)),
            pl.BlockSpec((s, LANES), lambda b, p, i: (b, p)),
            pl.BlockSpec((1, 2, 2, blk, blk), lambda b, p, i: (p, 0, 0, 0, 0)),
        ],
        out_specs=pl.BlockSpec((blk, LANES), lambda b, p, i: (b * nb + i, p)),
        out_shape=jax.ShapeDtypeStruct((t, a), BF16),
        scratch_shapes=[
            pltpu.VMEM((nb, LANES, blk), BF16),
            pltpu.VMEM((nbp, LANES), F32),
            pltpu.VMEM((2, nbp, blk), F32),
        ],
        compiler_params=_params("arbitrary", "arbitrary", "arbitrary"),
        name="moba_attention",
    )(cfar, q, k, v, bias_tables)


def _bias_tables(rel_bias):
    blk = MOBA_BLOCK
    n = np.float32(blk + 1)
    half = REL_BUCKETS // 2
    far_bucket = half + int(np.log(n / np.float32(half)) / math.log(REL_MAX_DIST / half) * (REL_BUCKETS - half))
    assert far_bucket >= REL_BUCKETS - 1
    bias_t = rel_bias.T.astype(F32)
    kk = jnp.arange(blk, dtype=jnp.int32)[:, None]
    qq = jnp.arange(blk, dtype=jnp.int32)[None, :]

    def lookup(bucket):
        out = jnp.zeros((bias_t.shape[0],) + bucket.shape, F32)
        for b in range(REL_BUCKETS):
            out = jnp.where(bucket[None] == b, bias_t[:, b][:, None, None], out)
        return out

    d_own = qq - kk
    own = jnp.where(d_own >= 0, lookup(_rel_bucket(d_own)), NEG_INF)
    prev = lookup(_rel_bucket(qq + blk - kk))
    tables = jnp.stack([own, prev], axis=1)
    tables = tables.reshape(N_HEADS // 2, 2, 2, blk, blk)
    cfar = bias_t[:, REL_BUCKETS - 1]
    return tables, cfar


def _mix_kernel(x_ref, h_ref, halo_ref, o_ref, sgc_ref, sga_ref, cw_ref, cb_ref, cg_ref, cbeta_ref,
                wcp_ref, wap_ref, wout_ref, g1_ref, b1_ref, *rest, tm, tiles_per_seq, alpha, n_experts):
    if n_experts:
        rw_ref, x1_ref, x1r_ref, route_ref, hc_scr = rest
    else:
        x1_ref, hc_scr = rest
    i = pl.program_id(0)
    first = (i % tiles_per_seq) == 0
    hc_scr[0:HALO_ROWS, :] = jnp.where(first, 0.0, halo_ref[...].astype(F32))
    hc_scr[HALO_ROWS:, :] = h_ref[...].astype(F32)
    base = HALO_ROWS - (CONV_KERNEL - 1)
    acc = cb_ref[...] + cw_ref[0:1, :] * hc_scr[base:base + tm, :]
    for kk in range(1, CONV_KERNEL):
        acc = acc + cw_ref[kk:kk + 1, :] * hc_scr[base + kk:base + kk + tm, :]
    hn = _layer_norm(acc, cg_ref[...], cbeta_ref[...])
    hs = (hn * _sigmoid(hn)).astype(BF16)
    y_conv = _dot(hs, wcp_ref[...])
    y_attn = _dot(o_ref[...], wap_ref[...])
    mixed = sgc_ref[...].astype(F32) * y_conv + sga_ref[...].astype(F32) * y_attn
    z = alpha * x_ref[...] + _dot(mixed.astype(BF16), wout_ref[...])
    x1 = _layer_norm(z, g1_ref[...], b1_ref[...])
    x1_ref[...] = x1
    if n_experts:
        nchunk = x1.shape[1] // LANES
        for cc in range(nchunk):
            x1r_ref[pl.ds(cc, tm, stride=nchunk), :] = x1[:, cc * LANES:(cc + 1) * LANES]
        logits = _dot(x1.astype(BF16), rw_ref[...])
        lane = lax.broadcasted_iota(jnp.int32, logits.shape, 1)
        lg = jnp.where(lane < n_experts, logits, NEG_INF)
        v1 = jnp.max(lg, axis=-1, keepdims=True)
        i1 = jnp.min(jnp.where(lg == v1, lane, LANES), axis=-1, keepdims=True)
        lg2 = jnp.where(lane == i1, NEG_INF, lg)
        v2 = jnp.max(lg2, axis=-1, keepdims=True)
        i2 = jnp.min(jnp.where(lg2 == v2, lane, LANES), axis=-1, keepdims=True)
        e = jnp.exp(v2 - v1)
        route_ref[...] = (jnp.where(lane == 0, i1.astype(F32), 0.0) + jnp.where(lane == 1, i2.astype(F32), 0.0)
                          + jnp.where(lane == 2, 1.0 / (1.0 + e), 0.0) + jnp.where(lane == 3, e / (1.0 + e), 0.0))


def _mix(x, h, o, sgc, sga, cw, cb, cg, cbeta, wcp, wap, wout, g1, b1, rw, *, s, tm, alpha, n_experts):
    t, d = x.shape
    c = h.shape[1]
    a = o.shape[1]
    row = lambda width: pl.BlockSpec((tm, width), lambda i: (i, 0))
    full = lambda arr: pl.BlockSpec(arr.shape, lambda i: (0,) * arr.ndim)
    halo_blocks = tm // HALO_ROWS
    in_specs = [row(d), row(c),
                pl.BlockSpec((HALO_ROWS, c), lambda i: (jnp.maximum(i * halo_blocks - 1, 0), 0)),
                row(a), row(d), row(d),
                full(cw), full(cb), full(cg), full(cbeta), full(wcp), full(wap), full(wout), full(g1), full(b1)]
    args = [x, h, h, o, sgc, sga, cw, cb, cg, cbeta, wcp, wap, wout, g1, b1]
    out_specs = [row(d)]
    out_shape = [jax.ShapeDtypeStruct((t, d), F32)]
    if n_experts:
        assert d == SUBLANES * LANES
        in_specs.append(full(rw))
        args.append(rw)
        out_specs += [pl.BlockSpec((tm * SUBLANES, LANES), lambda i: (i, 0)), row(LANES)]
        out_shape += [jax.ShapeDtypeStruct((t * SUBLANES, LANES), F32), jax.ShapeDtypeStruct((t, LANES), F32)]
    res = pl.pallas_call(
        functools.partial(_mix_kernel, tm=tm, tiles_per_seq=s // tm, alpha=alpha, n_experts=n_experts),
        grid=(t // tm,),
        in_specs=in_specs,
        out_specs=out_specs,
        out_shape=out_shape,
        scratch_shapes=[pltpu.VMEM((tm + HALO_ROWS, c), F32)],
        compiler_params=_params("arbitrary"),
        name="mix",
    )(*args)
    return res if n_experts else (res[0], None, None)


def _ple(xb, p_ref, wpg_ref, wpp_ref):
    return _sigmoid(_dot(xb, wpg_ref[...])) * _dot(p_ref[...].astype(BF16), wpp_ref[...])


def _ffn_kernel(x1_ref, p_ref, wg_ref, wu_ref, wd_ref, wpg_ref, wpp_ref, g2_ref, b2_ref, out_ref,
                xb_scr, acc_scr, *, alpha):
    j = pl.program_id(1)

    @pl.when(j == 0)
    def _():
        xb_scr[...] = x1_ref[...].astype(BF16)
        acc_scr[...] = jnp.zeros_like(acc_scr)

    xb = xb_scr[...]
    g = _dot(xb, wg_ref[...])
    u = _dot(xb, wu_ref[...])
    acc_scr[...] += _dot((g * _sigmoid(g) * u).astype(BF16), wd_ref[...])

    @pl.when(j == pl.num_programs(1) - 1)
    def _():
        ple = _ple(xb, p_ref, wpg_ref, wpp_ref)
        out_ref[...] = _layer_norm(alpha * x1_ref[...] + acc_scr[...] + ple, g2_ref[...], b2_ref[...])


def _ffn(x1, p, wg, wu, wd, wpg, wpp, g2, b2, *, tm, fc, alpha):
    t, d = x1.shape
    f = wg.shape[1]
    row = lambda width: pl.BlockSpec((tm, width), lambda i, j: (i, 0))
    full = lambda arr: pl.BlockSpec(arr.shape, lambda i, j: (0,) * arr.ndim)
    return pl.pallas_call(
        functools.partial(_ffn_kernel, alpha=alpha),
        grid=(t // tm, f // fc),
        in_specs=[row(d), row(p.shape[1]),
                  pl.BlockSpec((d, fc), lambda i, j: (0, j)),
                  pl.BlockSpec((d, fc), lambda i, j: (0, j)),
                  pl.BlockSpec((fc, d), lambda i, j: (j, 0)),
                  full(wpg), full(wpp), full(g2), full(b2)],
        out_specs=row(d),
        out_shape=jax.ShapeDtypeStruct((t, d), F32),
        scratch_shapes=[pltpu.VMEM((tm, d), BF16), pltpu.VMEM((tm, d), F32)],
        compiler_params=_params("arbitrary", "arbitrary"),
        name="ffn",
    )(x1, p, wg, wu, wd, wpg, wpp, g2, b2)


def _moe_plan(route, n_experts, tm):
    t = route.shape[0]
    e_slot = route[:, 0:TOP_K].astype(jnp.int32).reshape(-1)
    n_slots = e_slot.shape[0]
    ids = jnp.arange(n_experts, dtype=jnp.int32)
    onehot = (e_slot[:, None] == ids[None, :]).astype(jnp.int32)
    csum = jnp.cumsum(onehot, axis=0)
    cnt = csum[-1]
    rank = jnp.sum((csum - onehot) * onehot, axis=1)
    padded = ((cnt + tm - 1) // tm) * tm
    off_end = jnp.cumsum(padded)
    off = off_end - padded
    pos = off[e_slot] + rank
    n_tiles = n_slots // tm + n_experts
    tile_start = jnp.arange(n_tiles, dtype=jnp.int32) * tm
    tile_expert = jnp.minimum(jnp.sum((tile_start[:, None] >= off_end[None, :]).astype(jnp.int32), axis=1),
                              n_experts - 1)
    tile_valid = (tile_start < off_end[-1]).astype(jnp.int32)
    order = jnp.argsort(e_slot, stable=True).astype(jnp.int32)
    first_sorted = jnp.cumsum(cnt) - cnt
    r = jnp.arange(n_tiles * tm, dtype=jnp.int32)
    e_r = tile_expert[r // tm]
    w = r - off[e_r]
    u = jnp.clip(first_sorted[e_r] + w, 0, n_slots - 1)
    src = jnp.where(w < cnt[e_r], order[u] // TOP_K, 0)
    return tile_expert, tile_valid, src.reshape(n_tiles, 1, tm), pos


def _row_copy(src_hbm, src_row, dst_vmem, dst_row, sem):
    return pltpu.make_async_copy(
        src_hbm.at[pl.ds(pl.multiple_of(src_row * SUBLANES, SUBLANES), SUBLANES), :],
        dst_vmem.at[pl.ds(pl.multiple_of(dst_row * SUBLANES, SUBLANES), SUBLANES), :], sem)


def _moe_ffn_kernel(te_ref, valid_ref, src_ref, x1r_ref, wg_ref, wu_ref, wd_ref, y_ref,
                    xg_scr, xb_scr, acc_scr, sem, *, tm):
    i = pl.program_id(0)
    j = pl.program_id(1)
    valid = valid_ref[i] != 0
    nchunk = xb_scr.shape[1] // LANES

    @pl.when(valid & (j == 0))
    def _():
        def issue(r, carry):
            _row_copy(x1r_ref, src_ref[0, 0, r], xg_scr, r, sem).start()
            return carry

        lax.fori_loop(0, tm, issue, 0)

        def drain(r, carry):
            _row_copy(x1r_ref, 0, xg_scr, r, sem).wait()
            return carry

        lax.fori_loop(0, tm, drain, 0)
        for cc in range(nchunk):
            xb_scr[:, cc * LANES:(cc + 1) * LANES] = xg_scr[pl.ds(cc, tm, stride=nchunk), :].astype(BF16)
        acc_scr[...] = jnp.zeros_like(acc_scr)

    @pl.when(valid)
    def _():
        xb = xb_scr[...]
        g = _dot(xb, wg_ref[0])
        u = _dot(xb, wu_ref[0])
        acc_scr[...] += _dot((g * _sigmoid(g) * u).astype(BF16), wd_ref[0])

    last = j == pl.num_programs(1) - 1

    @pl.when(last & valid)
    def _():
        for cc in range(nchunk):
            y_ref[pl.ds(cc, tm, stride=nchunk), :] = acc_scr[:, cc * LANES:(cc + 1) * LANES]

    @pl.when(last & jnp.logical_not(valid))
    def _():
        y_ref[...] = jnp.zeros_like(y_ref)


def _moe_ffn(x1r, tile_expert, tile_valid, src, wg, wu, wd, *, tm, fc):
    ne, d, f = wg.shape
    n_tiles = src.shape[0]
    nj = f // fc
    jj = lambda i, j, te, tv: jnp.where(tv[i] != 0, j, nj - 1)
    return pl.pallas_call(
        functools.partial(_moe_ffn_kernel, tm=tm),
        grid_spec=pltpu.PrefetchScalarGridSpec(
            num_scalar_prefetch=2,
            grid=(n_tiles, nj),
            in_specs=[
                pl.BlockSpec((1, 1, tm), lambda i, j, te, tv: (i, 0, 0), memory_space=pltpu.SMEM),
                pl.BlockSpec(memory_space=pl.ANY),
                pl.BlockSpec((1, d, fc), lambda i, j, te, tv: (te[i], 0, jj(i, j, te, tv))),
                pl.BlockSpec((1, d, fc), lambda i, j, te, tv: (te[i], 0, jj(i, j, te, tv))),
                pl.BlockSpec((1, fc, d), lambda i, j, te, tv: (te[i], jj(i, j, te, tv), 0)),
            ],
            out_specs=pl.BlockSpec((tm * SUBLANES, LANES), lambda i, j, te, tv: (i, 0)),
            scratch_shapes=[pltpu.VMEM((tm * SUBLANES, LANES), F32), pltpu.VMEM((tm, d), BF16),
                            pltpu.VMEM((tm, d), F32), pltpu.SemaphoreType.DMA],
        ),
        out_shape=jax.ShapeDtypeStruct((n_tiles * tm * SUBLANES, LANES), F32),
        compiler_params=_params("arbitrary", "arbitrary"),
        name="moe_ffn",
    )(tile_expert, tile_valid, src, x1r, wg, wu, wd)


def _moe_combine_kernel(pos_ref, x1_ref, p_ref, route_ref, yr_ref, wpg_ref, wpp_ref, g2_ref, b2_ref, out_ref,
                        yg_scr, sem, *, tm, alpha):
    d = x1_ref.shape[1]
    nchunk = d // LANES

    def issue(tok, carry):
        for kk in range(TOP_K):
            _row_copy(yr_ref, pos_ref[0, 0, TOP_K * tok + kk], yg_scr, kk * tm + tok, sem).start()
        return carry

    lax.fori_loop(0, tm, issue, 0)
    x1 = x1_ref[...]
    ple = _ple(x1.astype(BF16), p_ref, wpg_ref, wpp_ref)

    def drain(tok, carry):
        for kk in range(TOP_K):
            _row_copy(yr_ref, 0, yg_scr, kk * tm + tok, sem).wait()
        return carry

    lax.fori_loop(0, tm, drain, 0)
    route = route_ref[...]
    gates = [route[:, TOP_K + kk:TOP_K + kk + 1] for kk in range(TOP_K)]
    chunks = []
    for cc in range(nchunk):
        f = gates[0] * yg_scr[pl.ds(cc, tm, stride=nchunk), :]
        for kk in range(1, TOP_K):
            f = f + gates[kk] * yg_scr[pl.ds(kk * tm * nchunk + cc, tm, stride=nchunk), :]
        chunks.append(f)
    ffn = jnp.concatenate(chunks, axis=1)
    out_ref[...] = _layer_norm(alpha * x1 + ffn + ple, g2_ref[...], b2_ref[...])


def _moe_combine(x1, p, route, yr, pos, wpg, wpp, g2, b2, *, tm, alpha):
    t, d = x1.shape
    row = lambda width: pl.BlockSpec((tm, width), lambda i: (i, 0))
    full = lambda arr: pl.BlockSpec(arr.shape, lambda i: (0,) * arr.ndim)
    pos3 = pos.reshape(t // tm, 1, TOP_K * tm)
    return pl.pallas_call(
        functools.partial(_moe_combine_kernel, tm=tm, alpha=alpha),
        grid=(t // tm,),
        in_specs=[pl.BlockSpec((1, 1, TOP_K * tm), lambda i: (i, 0, 0), memory_space=pltpu.SMEM),
                  row(d), row(p.shape[1]), row(LANES), pl.BlockSpec(memory_space=pl.ANY),
                  full(wpg), full(wpp), full(g2), full(b2)],
        out_specs=row(d),
        out_shape=jax.ShapeDtypeStruct((t, d), F32),
        scratch_shapes=[pltpu.VMEM((TOP_K * tm * SUBLANES, LANES), F32), pltpu.SemaphoreType.DMA],
        compiler_params=_params("arbitrary"),
        name="moe_combine",
    )(pos3, x1, p, route, yr, wpg, wpp, g2, b2)


def kernel(x, p, rel_bias, w_in, conv_w, conv_b, conv_ln_g, conv_ln_b, w_conv_proj, w_attn_proj, w_out, ln1_g, ln1_b, w_ple_gate, w_ple_proj, ln2_g, ln2_b, ffn_w_gate, ffn_w_up, ffn_w_down, router_w, exp_w_gate, exp_w_up, exp_w_down):
    bsz, s, d = x.shape
    depth = w_in.shape[0]
    c = conv_w.shape[2]
    a = w_attn_proj.shape[1]
    n_experts = router_w.shape[2]
    assert a == N_HEADS * HEAD_DIM and s % MOBA_BLOCK == 0 and conv_w.shape[1] == CONV_KERNEL
    assert n_experts >= TOP_K
    alpha = (2 * depth) ** 0.25
    t = bsz * s
    tm = 512
    tm_combine = 256
    fc = 512
    assert s % tm == 0 and (TOP_K * t) % tm == 0

    tables, cfar = _bias_tables(rel_bias)
    xf = x.reshape(t, d)
    row2 = lambda v: v.reshape(1, -1).astype(F32)
    for i in range(depth):
        h, q, k, v, sgc, sga = _in_proj(xf, w_in[i].astype(BF16), c=c, a=a, tm=tm)
        o = _attention(q, k, v, tables, cfar, bsz=bsz, s=s)
        moe = i % 2 == 1
        cw = jnp.pad(conv_w[i], ((0, HALO_ROWS - CONV_KERNEL), (0, 0)))
        rw = jnp.pad(router_w[i // 2], ((0, 0), (0, LANES - n_experts))).astype(BF16) if moe else None
        x1, x1r, route = _mix(xf, h, o, sgc, sga, cw, row2(conv_b[i]), row2(conv_ln_g[i]), row2(conv_ln_b[i]),
                              w_conv_proj[i].astype(BF16), w_attn_proj[i].astype(BF16), w_out[i].astype(BF16),
                              row2(ln1_g[i]), row2(ln1_b[i]), rw,
                              s=s, tm=tm, alpha=alpha, n_experts=n_experts if moe else 0)
        tail = (w_ple_gate[i].astype(BF16), w_ple_proj[i].astype(BF16), row2(ln2_g[i]), row2(ln2_b[i]))
        p_i = p[i].reshape(t, -1)
        if moe:
            tile_expert, tile_valid, src, pos = _moe_plan(route, n_experts, tm)
            yr = _moe_ffn(x1r, tile_expert, tile_valid, src, exp_w_gate[i // 2].astype(BF16),
                          exp_w_up[i // 2].astype(BF16), exp_w_down[i // 2].astype(BF16), tm=tm, fc=fc)
            xf = _moe_combine(x1, p_i, route, yr, pos, *tail, tm=tm_combine, alpha=alpha)
        else:
            xf = _ffn(x1, p_i, ffn_w_gate[i // 2].astype(BF16), ffn_w_up[i // 2].astype(BF16),
                      ffn_w_down[i // 2].astype(BF16), *tail, tm=tm, fc=fc, alpha=alpha)
    return xf.reshape(bsz, s, d)
```

```python
import functools
import math

import numpy as np
import jax
import jax.numpy as jnp
from jax import lax
from jax.experimental import pallas as pl
from jax.experimental.pallas import tpu as pltpu

N_HEADS = 8
HEAD_DIM = 64
CONV_KERNEL = 31
MOBA_BLOCK = 256
MOBA_TOPK = 3
REL_BUCKETS = 32
REL_MAX_DIST = 128
TOP_K = 2
LN_EPS = 1e-5

LANES = 128
SUBLANES = 8
BF16_SUBLANES = 16
HALO_ROWS = 32
VMEM_LIMIT = 48 * 1024 * 1024

F32 = jnp.float32
BF16 = jnp.bfloat16
NEG_INF = float("-inf")


def _sigmoid(t):
    return 1.0 / (1.0 + jnp.exp(-t))


def _layer_norm(t, g, b):
    mu = jnp.mean(t, axis=-1, keepdims=True)
    d = t - mu
    var = jnp.mean(d * d, axis=-1, keepdims=True)
    return d * lax.rsqrt(var + LN_EPS) * g + b


def _dot(a, b):
    return jnp.dot(a, b, preferred_element_type=F32)


def _params(*semantics):
    return pltpu.CompilerParams(dimension_semantics=semantics, vmem_limit_bytes=VMEM_LIMIT)


def _rel_bucket(dist):
    n = jnp.maximum(dist, 0)
    max_exact = REL_BUCKETS // 2
    nf = jnp.maximum(n, 1).astype(F32)
    large = max_exact + (jnp.log(nf / max_exact) / math.log(REL_MAX_DIST / max_exact)
                         * (REL_BUCKETS - max_exact)).astype(jnp.int32)
    large = jnp.minimum(large, REL_BUCKETS - 1)
    return jnp.where(n < max_exact, n, large)


def _in_proj_kernel(x_ref, w_ref, h_ref, q_ref, k_ref, v_ref, sgc_ref, sga_ref, *, c, a, d):
    xb = x_ref[...].astype(BF16)
    o = 0
    glu_in = _dot(xb, w_ref[:, o:o + c])
    glu_gate = _dot(xb, w_ref[:, o + c:o + 2 * c])
    h_ref[...] = (glu_in * _sigmoid(glu_gate)).astype(BF16)
    o += 2 * c
    q_ref[...] = _dot(xb, w_ref[:, o:o + a]).astype(BF16)
    o += a
    k_ref[...] = _dot(xb, w_ref[:, o:o + a]).astype(BF16)
    o += a
    v_ref[...] = _dot(xb, w_ref[:, o:o + a]).astype(BF16)
    o += a
    sgc_ref[...] = _sigmoid(_dot(xb, w_ref[:, o:o + d])).astype(BF16)
    o += d
    sga_ref[...] = _sigmoid(_dot(xb, w_ref[:, o:o + d])).astype(BF16)


def _in_proj(x, w, *, c, a, tm):
    t, d = x.shape
    n = w.shape[1]
    row = lambda width: pl.BlockSpec((tm, width), lambda i: (i, 0))
    return pl.pallas_call(
        functools.partial(_in_proj_kernel, c=c, a=a, d=d),
        grid=(t // tm,),
        in_specs=[row(d), pl.BlockSpec((d, n), lambda i: (0, 0))],
        out_specs=[row(c), row(a), row(a), row(a), row(d), row(d)],
        out_shape=[jax.ShapeDtypeStruct((t, c), BF16)] + [jax.ShapeDtypeStruct((t, a), BF16)] * 3
        + [jax.ShapeDtypeStruct((t, d), BF16)] * 2,
        compiler_params=_params("arbitrary"),
        name="in_proj",
    )(x, w)


LOG2E = math.log2(math.e)
MASKED = float(jnp.finfo(jnp.bfloat16).min)
VT_ROWS = HEAD_DIM + BF16_SUBLANES


def _attn_kernel(q_ref, k_ref, v_ref, bias_ref, o_ref, vt_scr, kaug_scr, kmean_scr, sel_scr, sa_scr, sb_scr,
                 *, nb, nbp):
    blk = MOBA_BLOCK
    i = pl.program_id(2)

    @pl.when(i == 0)
    def _():
        if nbp > nb:
            kmean_scr[...] = jnp.zeros_like(kmean_scr)
        col = lax.broadcasted_iota(jnp.int32, (blk, LANES), 1)
        ones = jnp.ones((BF16_SUBLANES, blk), BF16)
        for j in range(nb):
            kb = k_ref[j * blk:(j + 1) * blk, :]
            kmean_scr[j:j + 1, :] = jnp.sum(kb.astype(F32), axis=0, keepdims=True) * (1.0 / blk)
            kaug_scr[j * blk:(j + 1) * blk, 0:LANES] = kb
            kaug_scr[j * blk:(j + 1) * blk, LANES:2 * LANES] = jnp.where(col == j, 1.0, 0.0).astype(BF16)
            vt = v_ref[j * blk:(j + 1) * blk, :].astype(F32).T.astype(BF16)
            for hh in range(2):
                vt_scr[j, hh, 0:HEAD_DIM, :] = vt[hh * HEAD_DIM:(hh + 1) * HEAD_DIM, :]
                vt_scr[j, hh, HEAD_DIM:VT_ROWS, :] = ones

    qt = q_ref[...].astype(F32).T * (HEAD_DIM ** -0.5 * LOG2E)
    qrow = lax.broadcasted_iota(jnp.int32, qt.shape, 0)
    kmean = kmean_scr[...].astype(BF16)
    brow = lax.broadcasted_iota(jnp.int32, (nbp, blk), 0)
    n_far = i - 1
    pad_rows = jnp.zeros((LANES - nbp, blk), BF16)

    qhs, qaugs = [], []
    for hh in range(2):
        lo = hh * HEAD_DIM
        qh = jnp.where((qrow >= lo) & (qrow < lo + HEAD_DIM), qt, 0.0).astype(BF16)
        qhs.append(qh)
        score = _dot(kmean, qh)
        cur = jnp.where(brow < i, score, NEG_INF)
        sel = jnp.zeros((nbp, blk), dtype=jnp.bool_)
        for t in range(MOBA_TOPK):
            mx = jnp.max(cur, axis=0, keepdims=True)
            first = jnp.min(jnp.where(cur == mx, brow, nbp), axis=0, keepdims=True)
            pick = brow == first
            sel = sel | (pick & (t < i))
            cur = jnp.where(pick, NEG_INF, cur)
        sel_scr[hh] = jnp.where(sel, 0.0, NEG_INF)
        far_rows = jnp.where(sel & (brow < n_far), 0.0, MASKED).astype(BF16)
        qaugs.append(jnp.concatenate([qh, far_rows, pad_rows], axis=0))

    def k_block(j):
        return k_ref[pl.ds(pl.multiple_of(j * blk, blk), blk), :]

    def sel_row(j, hh):
        return sel_scr[hh, pl.ds(j, 1), :]

    jp = jnp.maximum(i - 1, 0)
    n_units = (n_far + 1) // 2
    u_last = nb // 2 - 1

    def stage_ab_near(s_buf):
        k_prev, k_own = k_block(jp), k_block(i)
        cms = []
        for hh in range(2):
            s0 = _dot(k_prev, qhs[hh]) + bias_ref[0, hh, 1] + sel_row(jp, hh)
            s1 = _dot(k_own, qhs[hh]) + bias_ref[0, hh, 0]
            s_buf[hh, 0:blk, :] = s0
            s_buf[hh, blk:2 * blk, :] = s1
            cms.append(jnp.maximum(jnp.max(s0, axis=0, keepdims=True), jnp.max(s1, axis=0, keepdims=True)))
        return cms

    def stage_ab(u, s_buf):
        u = jnp.minimum(u, u_last)
        kaug = kaug_scr[pl.ds(pl.multiple_of(u * (2 * blk), 2 * blk), 2 * blk), :]
        cms = []
        for hh in range(2):
            s0 = _dot(kaug[0:blk], qaugs[hh])
            s1 = _dot(kaug[blk:2 * blk], qaugs[hh])
            s_buf[hh, 0:blk, :] = s0
            s_buf[hh, blk:2 * blk, :] = s1
            cms.append(jnp.maximum(jnp.max(s0, axis=0, keepdims=True), jnp.max(s1, axis=0, keepdims=True)))
        return cms

    def stage_c(j0, j1, s_buf, cms, carry):
        out = []
        for hh in range(2):
            m_old, acc_old = carry[2 * hh:2 * hh + 2]
            m_new = jnp.maximum(m_old, cms[hh])
            p0 = jnp.exp2(s_buf[hh, 0:blk, :] - m_new).astype(BF16)
            p1 = jnp.exp2(s_buf[hh, blk:2 * blk, :] - m_new).astype(BF16)
            acc_new = (jnp.exp2(m_old - m_new) * acc_old + _dot(vt_scr[j0, hh], p0) + _dot(vt_scr[j1, hh], p1))
            out += [m_new, acc_new]
        return out

    def stage_c_far(u, s_buf, cms, carry):
        u = jnp.minimum(u, u_last)
        return stage_c(2 * u, 2 * u + 1, s_buf, cms, carry)

    cm_near = stage_ab_near(sa_scr)
    cm_b = stage_ab(0, sb_scr)
    carry = []
    for hh in range(2):
        carry += [jnp.full((1, blk), NEG_INF, F32), jnp.zeros((VT_ROWS, blk), F32)]
    carry = stage_c(jp, i, sa_scr, cm_near, carry)

    def far_body(it, state):
        cm_b, carry = list(state[0:2]), list(state[2:])
        u0 = 2 * it
        cm_a = stage_ab(u0 + 1, sa_scr)
        carry = stage_c_far(u0, sb_scr, cm_b, carry)
        cm_b = stage_ab(u0 + 2, sb_scr)
        carry = stage_c_far(u0 + 1, sa_scr, cm_a, carry)
        return tuple(cm_b + carry)

    state = lax.fori_loop(0, (n_units + 1) // 2, far_body, tuple(cm_b + carry))
    carry = state[2:]
    outs = [carry[2 * hh + 1][0:HEAD_DIM] / carry[2 * hh + 1][HEAD_DIM:HEAD_DIM + 1] for hh in range(2)]
    o_ref[...] = jnp.concatenate(outs, axis=0).T.astype(BF16)


def _attention(q, k, v, bias_tables, *, bsz, s):
    t, a = q.shape
    blk = MOBA_BLOCK
    nb = s // blk
    nbp = -(-nb // BF16_SUBLANES) * BF16_SUBLANES
    assert nbp <= LANES and nb % 2 == 0
    npairs = a // LANES
    return pl.pallas_call(
        functools.partial(_attn_kernel, nb=nb, nbp=nbp),
        grid=(bsz, npairs, nb),
        in_specs=[
            pl.BlockSpec((blk, LANES), lambda b, p, i: (b * nb + i, p)),
            pl.BlockSpec((s, LANES), lambda b, p, i: (b, p)),
            pl.BlockSpec((s, LANES), lambda b, p, i: (b, p)),
            pl.BlockSpec((1, 2, 2, blk, blk), lambda b, p, i: (p, 0, 0, 0, 0)),
        ],
        out_specs=pl.BlockSpec((blk, LANES), lambda b, p, i: (b * nb + i, p)),
        out_shape=jax.ShapeDtypeStruct((t, a), BF16),
        scratch_shapes=[
            pltpu.VMEM((nb, 2, VT_ROWS, blk), BF16),
            pltpu.VMEM((nb * blk, 2 * LANES), BF16),
            pltpu.VMEM((nbp, LANES), F32),
            pltpu.VMEM((2, nbp, blk), F32),
            pltpu.VMEM((2, 2 * blk, blk), F32),
            pltpu.VMEM((2, 2 * blk, blk), F32),
        ],
        compiler_params=_params("arbitrary", "arbitrary", "arbitrary"),
        name="moba_attention",
    )(q, k, v, bias_tables)


def _bias_tables(rel_bias):
    blk = MOBA_BLOCK
    n = np.float32(blk + 1)
    half = REL_BUCKETS // 2
    far_bucket = half + int(np.log(n / np.float32(half)) / math.log(REL_MAX_DIST / half) * (REL_BUCKETS - half))
    assert far_bucket >= REL_BUCKETS - 1
    bias_t = rel_bias.T.astype(F32)
    bias_t = (bias_t - bias_t[:, REL_BUCKETS - 1:]) * LOG2E
    kk = jnp.arange(blk, dtype=jnp.int32)[:, None]
    qq = jnp.arange(blk, dtype=jnp.int32)[None, :]

    def lookup(bucket):
        out = jnp.zeros((bias_t.shape[0],) + bucket.shape, F32)
        for b in range(REL_BUCKETS):
            out = jnp.where(bucket[None] == b, bias_t[:, b][:, None, None], out)
        return out

    d_own = qq - kk
    own = jnp.where(d_own >= 0, lookup(_rel_bucket(d_own)), NEG_INF)
    prev = lookup(_rel_bucket(qq + blk - kk))
    tables = jnp.stack([own, prev], axis=1)
    return tables.reshape(N_HEADS // 2, 2, 2, blk, blk)


def _mix_kernel(x_ref, h_ref, halo_ref, o_ref, sgc_ref, sga_ref, cw_ref, cb_ref, cg_ref, cbeta_ref,
                wcp_ref, wap_ref, wout_ref, g1_ref, b1_ref, *rest, tm, tiles_per_seq, alpha, n_experts):
    if n_experts:
        rw_ref, x1_ref, x1r_ref, route_ref, hc_scr = rest
    else:
        x1_ref, hc_scr = rest
    i = pl.program_id(0)
    first = (i % tiles_per_seq) == 0
    hc_scr[0:HALO_ROWS, :] = jnp.where(first, 0.0, halo_ref[...].astype(F32))
    hc_scr[HALO_ROWS:, :] = h_ref[...].astype(F32)
    base = HALO_ROWS - (CONV_KERNEL - 1)
    acc = cb_ref[...] + cw_ref[0:1, :] * hc_scr[base:base + tm, :]
    for kk in range(1, CONV_KERNEL):
        acc = acc + cw_ref[kk:kk + 1, :] * hc_scr[base + kk:base + kk + tm, :]
    hn = _layer_norm(acc, cg_ref[...], cbeta_ref[...])
    hs = (hn * _sigmoid(hn)).astype(BF16)
    y_conv = _dot(hs, wcp_ref[...])
    y_attn = _dot(o_ref[...], wap_ref[...])
    mixed = sgc_ref[...].astype(F32) * y_conv + sga_ref[...].astype(F32) * y_attn
    z = alpha * x_ref[...] + _dot(mixed.astype(BF16), wout_ref[...])
    x1 = _layer_norm(z, g1_ref[...], b1_ref[...])
    x1_ref[...] = x1
    if n_experts:
        nchunk = x1.shape[1] // LANES
        for cc in range(nchunk):
            x1r_ref[pl.ds(cc, tm, stride=nchunk), :] = x1[:, cc * LANES:(cc + 1) * LANES]
        logits = _dot(x1.astype(BF16), rw_ref[...])
        lane = lax.broadcasted_iota(jnp.int32, logits.shape, 1)
        lg = jnp.where(lane < n_experts, logits, NEG_INF)
        v1 = jnp.max(lg, axis=-1, keepdims=True)
        i1 = jnp.min(jnp.where(lg == v1, lane, LANES), axis=-1, keepdims=True)
        lg2 = jnp.where(lane == i1, NEG_INF, lg)
        v2 = jnp.max(lg2, axis=-1, keepdims=True)
        i2 = jnp.min(jnp.where(lg2 == v2, lane, LANES), axis=-1, keepdims=True)
        e = jnp.exp(v2 - v1)
        route_ref[...] = (jnp.where(lane == 0, i1.astype(F32), 0.0) + jnp.where(lane == 1, i2.astype(F32), 0.0)
                          + jnp.where(lane == 2, 1.0 / (1.0 + e), 0.0) + jnp.where(lane == 3, e / (1.0 + e), 0.0))


def _mix(x, h, o, sgc, sga, cw, cb, cg, cbeta, wcp, wap, wout, g1, b1, rw, *, s, tm, alpha, n_experts):
    t, d = x.shape
    c = h.shape[1]
    a = o.shape[1]
    row = lambda width: pl.BlockSpec((tm, width), lambda i: (i, 0))
    full = lambda arr: pl.BlockSpec(arr.shape, lambda i: (0,) * arr.ndim)
    halo_blocks = tm // HALO_ROWS
    in_specs = [row(d), row(c),
                pl.BlockSpec((HALO_ROWS, c), lambda i: (jnp.maximum(i * halo_blocks - 1, 0), 0)),
                row(a), row(d), row(d),
                full(cw), full(cb), full(cg), full(cbeta), full(wcp), full(wap), full(wout), full(g1), full(b1)]
    args = [x, h, h, o, sgc, sga, cw, cb, cg, cbeta, wcp, wap, wout, g1, b1]
    out_specs = [row(d)]
    out_shape = [jax.ShapeDtypeStruct((t, d), F32)]
    if n_experts:
        assert d == SUBLANES * LANES
        in_specs.append(full(rw))
        args.append(rw)
        out_specs += [pl.BlockSpec((tm * SUBLANES, LANES), lambda i: (i, 0)), row(LANES)]
        out_shape += [jax.ShapeDtypeStruct((t * SUBLANES, LANES), F32), jax.ShapeDtypeStruct((t, LANES), F32)]
    res = pl.pallas_call(
        functools.partial(_mix_kernel, tm=tm, tiles_per_seq=s // tm, alpha=alpha, n_experts=n_experts),
        grid=(t // tm,),
        in_specs=in_specs,
        out_specs=out_specs,
        out_shape=out_shape,
        scratch_shapes=[pltpu.VMEM((tm + HALO_ROWS, c), F32)],
        compiler_params=_params("arbitrary"),
        name="mix",
    )(*args)
    return res if n_experts else (res[0], None, None)


def _ple(xb, p_ref, wpg_ref, wpp_ref):
    return _sigmoid(_dot(xb, wpg_ref[...])) * _dot(p_ref[...].astype(BF16), wpp_ref[...])


def _ffn_kernel(x1_ref, p_ref, wg_ref, wu_ref, wd_ref, wpg_ref, wpp_ref, g2_ref, b2_ref, out_ref,
                xb_scr, acc_scr, *, alpha):
    j = pl.program_id(1)

    @pl.when(j == 0)
    def _():
        xb_scr[...] = x1_ref[...].astype(BF16)
        acc_scr[...] = jnp.zeros_like(acc_scr)

    xb = xb_scr[...]
    g = _dot(xb, wg_ref[...])
    u = _dot(xb, wu_ref[...])
    acc_scr[...] += _dot((g * _sigmoid(g) * u).astype(BF16), wd_ref[...])

    @pl.when(j == pl.num_programs(1) - 1)
    def _():
        ple = _ple(xb, p_ref, wpg_ref, wpp_ref)
        out_ref[...] = _layer_norm(alpha * x1_ref[...] + acc_scr[...] + ple, g2_ref[...], b2_ref[...])


def _ffn(x1, p, wg, wu, wd, wpg, wpp, g2, b2, *, tm, fc, alpha):
    t, d = x1.shape
    f = wg.shape[1]
    row = lambda width: pl.BlockSpec((tm, width), lambda i, j: (i, 0))
    full = lambda arr: pl.BlockSpec(arr.shape, lambda i, j: (0,) * arr.ndim)
    return pl.pallas_call(
        functools.partial(_ffn_kernel, alpha=alpha),
        grid=(t // tm, f // fc),
        in_specs=[row(d), row(p.shape[1]),
                  pl.BlockSpec((d, fc), lambda i, j: (0, j)),
                  pl.BlockSpec((d, fc), lambda i, j: (0, j)),
                  pl.BlockSpec((fc, d), lambda i, j: (j, 0)),
                  full(wpg), full(wpp), full(g2), full(b2)],
        out_specs=row(d),
        out_shape=jax.ShapeDtypeStruct((t, d), F32),
        scratch_shapes=[pltpu.VMEM((tm, d), BF16), pltpu.VMEM((tm, d), F32)],
        compiler_params=_params("arbitrary", "arbitrary"),
        name="ffn",
    )(x1, p, wg, wu, wd, wpg, wpp, g2, b2)


DMA_UNROLL = 8


def _moe_plan(route, n_experts, tm):
    e_slot = route[:, 0:TOP_K].astype(jnp.int32).reshape(-1)
    n_slots = e_slot.shape[0]
    ids = jnp.arange(n_experts, dtype=jnp.int32)
    onehot = (e_slot[:, None] == ids[None, :]).astype(jnp.int32)
    csum = jnp.cumsum(onehot, axis=0)
    cnt = csum[-1]
    padded = ((cnt + tm - 1) // tm) * tm
    off_end = jnp.cumsum(padded)
    off = off_end - padded
    pos = jnp.sum((csum - onehot + off[None, :]) * onehot, axis=1)
    n_tiles = n_slots // tm + n_experts
    tile_start = jnp.arange(n_tiles, dtype=jnp.int32) * tm
    tile_expert = jnp.minimum(jnp.sum((tile_start[:, None] >= off_end[None, :]).astype(jnp.int32), axis=1),
                              n_experts - 1)
    tile_valid = (tile_start < off_end[-1]).astype(jnp.int32)
    return tile_expert, tile_valid, pos, n_tiles


def _row_copy(src, src_row, dst, dst_row, sem):
    return pltpu.make_async_copy(
        src.at[pl.ds(pl.multiple_of(src_row * SUBLANES, SUBLANES), SUBLANES), :],
        dst.at[pl.ds(pl.multiple_of(dst_row * SUBLANES, SUBLANES), SUBLANES), :], sem)


def _moe_dispatch_kernel(pos_ref, x1r_ref, init_ref, xs_ref, sem, *, tm):
    del init_ref
    base = pl.program_id(0) * tm

    def copy(tok, kk):
        return _row_copy(x1r_ref, base + tok, xs_ref, pos_ref[0, 0, TOP_K * tok + kk], sem)

    def issue(tok, carry):
        for kk in range(TOP_K):
            copy(tok, kk).start()
        return carry

    lax.fori_loop(0, tm, issue, 0, unroll=DMA_UNROLL)

    def drain(tok, carry):
        for kk in range(TOP_K):
            copy(tok, kk).wait()
        return carry

    lax.fori_loop(0, tm, drain, 0, unroll=DMA_UNROLL)


def _moe_dispatch(x1r, pos, n_rows, *, tm):
    t = x1r.shape[0] // SUBLANES
    pos3 = pos.reshape(t // tm, 1, TOP_K * tm)
    init = jnp.zeros((n_rows * SUBLANES, LANES), F32)
    return pl.pallas_call(
        functools.partial(_moe_dispatch_kernel, tm=tm),
        grid=(t // tm,),
        in_specs=[pl.BlockSpec((1, 1, TOP_K * tm), lambda i: (i, 0, 0), memory_space=pltpu.SMEM),
                  pl.BlockSpec(memory_space=pl.ANY), pl.BlockSpec(memory_space=pl.ANY)],
        out_specs=pl.BlockSpec(memory_space=pl.ANY),
        out_shape=jax.ShapeDtypeStruct(init.shape, F32),
        scratch_shapes=[pltpu.SemaphoreType.DMA],
        input_output_aliases={2: 0},
        compiler_params=_params("arbitrary"),
        name="moe_dispatch",
    )(pos3, x1r, init)


def _moe_ffn_kernel(te_ref, valid_ref, xs_ref, wg_ref, wu_ref, wd_ref, y_ref, xb_scr, acc_scr, *, tm):
    i = pl.program_id(0)
    j = pl.program_id(1)
    valid = valid_ref[i] != 0
    nchunk = xb_scr.shape[1] // LANES

    @pl.when(valid & (j == 0))
    def _():
        for cc in range(nchunk):
            xb_scr[:, cc * LANES:(cc + 1) * LANES] = xs_ref[pl.ds(cc, tm, stride=nchunk), :].astype(BF16)
        acc_scr[...] = jnp.zeros_like(acc_scr)

    @pl.when(valid)
    def _():
        xb = xb_scr[...]
        g = _dot(xb, wg_ref[0])
        u = _dot(xb, wu_ref[0])
        acc_scr[...] += _dot((g * _sigmoid(g) * u).astype(BF16), wd_ref[0])

    last = j == pl.num_programs(1) - 1

    @pl.when(last & valid)
    def _():
        for cc in range(nchunk):
            y_ref[pl.ds(cc, tm, stride=nchunk), :] = acc_scr[:, cc * LANES:(cc + 1) * LANES]

    @pl.when(last & jnp.logical_not(valid))
    def _():
        y_ref[...] = jnp.zeros_like(y_ref)


def _moe_ffn(xs, tile_expert, tile_valid, wg, wu, wd, *, tm, fc):
    ne, d, f = wg.shape
    n_tiles = xs.shape[0] // (tm * SUBLANES)
    nj = f // fc
    jj = lambda i, j, te, tv: jnp.where(tv[i] != 0, j, nj - 1)
    rows = pl.BlockSpec((tm * SUBLANES, LANES), lambda i, j, te, tv: (i, 0))
    return pl.pallas_call(
        functools.partial(_moe_ffn_kernel, tm=tm),
        grid_spec=pltpu.PrefetchScalarGridSpec(
            num_scalar_prefetch=2,
            grid=(n_tiles, nj),
            in_specs=[
                rows,
                pl.BlockSpec((1, d, fc), lambda i, j, te, tv: (te[i], 0, jj(i, j, te, tv))),
                pl.BlockSpec((1, d, fc), lambda i, j, te, tv: (te[i], 0, jj(i, j, te, tv))),
                pl.BlockSpec((1, fc, d), lambda i, j, te, tv: (te[i], jj(i, j, te, tv), 0)),
            ],
            out_specs=rows,
            scratch_shapes=[pltpu.VMEM((tm, d), BF16), pltpu.VMEM((tm, d), F32)],
        ),
        out_shape=jax.ShapeDtypeStruct(xs.shape, F32),
        compiler_params=_params("arbitrary", "arbitrary"),
        name="moe_ffn",
    )(tile_expert, tile_valid, xs, wg, wu, wd)


def _moe_combine_kernel(pos_ref, x1_ref, p_ref, route_ref, yr_ref, wpg_ref, wpp_ref, g2_ref, b2_ref, out_ref,
                        yg_scr, sem, *, tm, alpha):
    d = x1_ref.shape[1]
    nchunk = d // LANES

    def copy(tok, kk):
        return _row_copy(yr_ref, pos_ref[0, 0, TOP_K * tok + kk], yg_scr, kk * tm + tok, sem)

    def issue(tok, carry):
        for kk in range(TOP_K):
            copy(tok, kk).start()
        return carry

    lax.fori_loop(0, tm, issue, 0, unroll=DMA_UNROLL)
    x1 = x1_ref[...]
    ple = _ple(x1.astype(BF16), p_ref, wpg_ref, wpp_ref)

    def drain(tok, carry):
        for kk in range(TOP_K):
            copy(tok, kk).wait()
        return carry

    lax.fori_loop(0, tm, drain, 0, unroll=DMA_UNROLL)
    route = route_ref[...]
    gates = [route[:, TOP_K + kk:TOP_K + kk + 1] for kk in range(TOP_K)]
    chunks = []
    for cc in range(nchunk):
        f = gates[0] * yg_scr[pl.ds(cc, tm, stride=nchunk), :]
        for kk in range(1, TOP_K):
            f = f + gates[kk] * yg_scr[pl.ds(kk * tm * nchunk + cc, tm, stride=nchunk), :]
        chunks.append(f)
    ffn = jnp.concatenate(chunks, axis=1)
    out_ref[...] = _layer_norm(alpha * x1 + ffn + ple, g2_ref[...], b2_ref[...])


def _moe_combine(x1, p, route, yr, pos, wpg, wpp, g2, b2, *, tm, alpha):
    t, d = x1.shape
    row = lambda width: pl.BlockSpec((tm, width), lambda i: (i, 0))
    full = lambda arr: pl.BlockSpec(arr.shape, lambda i: (0,) * arr.ndim)
    pos3 = pos.reshape(t // tm, 1, TOP_K * tm)
    return pl.pallas_call(
        functools.partial(_moe_combine_kernel, tm=tm, alpha=alpha),
        grid=(t // tm,),
        in_specs=[pl.BlockSpec((1, 1, TOP_K * tm), lambda i: (i, 0, 0), memory_space=pltpu.SMEM),
                  row(d), row(p.shape[1]), row(LANES), pl.BlockSpec(memory_space=pl.ANY),
                  full(wpg), full(wpp), full(g2), full(b2)],
        out_specs=row(d),
        out_shape=jax.ShapeDtypeStruct((t, d), F32),
        scratch_shapes=[pltpu.VMEM((TOP_K * tm * SUBLANES, LANES), F32), pltpu.SemaphoreType.DMA],
        compiler_params=_params("arbitrary"),
        name="moe_combine",
    )(pos3, x1, p, route, yr, wpg, wpp, g2, b2)


def kernel(x, p, rel_bias, w_in, conv_w, conv_b, conv_ln_g, conv_ln_b, w_conv_proj, w_attn_proj, w_out, ln1_g, ln1_b, w_ple_gate, w_ple_proj, ln2_g, ln2_b, ffn_w_gate, ffn_w_up, ffn_w_down, router_w, exp_w_gate, exp_w_up, exp_w_down):
    bsz, s, d = x.shape
    depth = w_in.shape[0]
    c = conv_w.shape[2]
    a = w_attn_proj.shape[1]
    n_experts = router_w.shape[2]
    assert a == N_HEADS * HEAD_DIM and s % (2 * MOBA_BLOCK) == 0 and conv_w.shape[1] == CONV_KERNEL
    assert n_experts >= TOP_K
    alpha = (2 * depth) ** 0.25
    t = bsz * s
    tm = 512
    tm_combine = 256
    fc = 512
    assert s % tm == 0 and (TOP_K * t) % tm == 0

    tables = _bias_tables(rel_bias)
    xf = x.reshape(t, d)
    row2 = lambda v: v.reshape(1, -1).astype(F32)
    for i in range(depth):
        h, q, k, v, sgc, sga = _in_proj(xf, w_in[i].astype(BF16), c=c, a=a, tm=tm)
        o = _attention(q, k, v, tables, bsz=bsz, s=s)
        moe = i % 2 == 1
        cw = jnp.pad(conv_w[i], ((0, HALO_ROWS - CONV_KERNEL), (0, 0)))
        rw = jnp.pad(router_w[i // 2], ((0, 0), (0, LANES - n_experts))).astype(BF16) if moe else None
        x1, x1r, route = _mix(xf, h, o, sgc, sga, cw, row2(conv_b[i]), row2(conv_ln_g[i]), row2(conv_ln_b[i]),
                              w_conv_proj[i].astype(BF16), w_attn_proj[i].astype(BF16), w_out[i].astype(BF16),
                              row2(ln1_g[i]), row2(ln1_b[i]), rw,
                              s=s, tm=tm, alpha=alpha, n_experts=n_experts if moe else 0)
        tail = (w_ple_gate[i].astype(BF16), w_ple_proj[i].astype(BF16), row2(ln2_g[i]), row2(ln2_b[i]))
        p_i = p[i].reshape(t, -1)
        if moe:
            tile_expert, tile_valid, pos, n_tiles = _moe_plan(route, n_experts, tm)
            xs = _moe_dispatch(x1r, pos, n_tiles * tm, tm=tm)
            yr = _moe_ffn(xs, tile_expert, tile_valid, exp_w_gate[i // 2].astype(BF16),
                          exp_w_up[i // 2].astype(BF16), exp_w_down[i // 2].astype(BF16), tm=tm, fc=fc)
            xf = _moe_combine(x1, p_i, route, yr, pos, *tail, tm=tm_combine, alpha=alpha)
        else:
            xf = _ffn(x1, p_i, ffn_w_gate[i // 2].astype(BF16), ffn_w_up[i // 2].astype(BF16),
                      ffn_w_down[i // 2].astype(BF16), *tail, tm=tm, fc=fc, alpha=alpha)
    return xf.reshape(bsz, s, d)
```

```python
import functools
import math

import numpy as np
import jax
import jax.numpy as jnp
from jax import lax
from jax.experimental import pallas as pl
from jax.experimental.pallas import tpu as pltpu

N_HEADS = 8
HEAD_DIM = 64
CONV_KERNEL = 31
MOBA_BLOCK = 256
MOBA_TOPK = 3
REL_BUCKETS = 32
REL_MAX_DIST = 128
TOP_K = 2
LN_EPS = 1e-5

LANES = 128
SUBLANES = 8
BF16_SUBLANES = 16
HALO_ROWS = 32
VMEM_LIMIT = 48 * 1024 * 1024

F32 = jnp.float32
BF16 = jnp.bfloat16
NEG_INF = float("-inf")


def _sigmoid(t):
    return 1.0 / (1.0 + jnp.exp(-t))


def _layer_norm(t, g, b):
    mu = jnp.mean(t, axis=-1, keepdims=True)
    d = t - mu
    var = jnp.mean(d * d, axis=-1, keepdims=True)
    return d * lax.rsqrt(var + LN_EPS) * g + b


def _dot(a, b):
    return jnp.dot(a, b, preferred_element_type=F32)


def _params(*semantics):
    return pltpu.CompilerParams(dimension_semantics=semantics, vmem_limit_bytes=VMEM_LIMIT)


def _rel_bucket(dist):
    n = jnp.maximum(dist, 0)
    max_exact = REL_BUCKETS // 2
    nf = jnp.maximum(n, 1).astype(F32)
    large = max_exact + (jnp.log(nf / max_exact) / math.log(REL_MAX_DIST / max_exact)
                         * (REL_BUCKETS - max_exact)).astype(jnp.int32)
    large = jnp.minimum(large, REL_BUCKETS - 1)
    return jnp.where(n < max_exact, n, large)


def _in_proj_kernel(x_ref, w_ref, h_ref, q_ref, k_ref, v_ref, sgc_ref, sga_ref, *, c, a, d):
    xb = x_ref[...].astype(BF16)
    o = 0
    glu_in = _dot(xb, w_ref[:, o:o + c])
    glu_gate = _dot(xb, w_ref[:, o + c:o + 2 * c])
    h_ref[...] = (glu_in * _sigmoid(glu_gate)).astype(BF16)
    o += 2 * c
    q_ref[...] = _dot(xb, w_ref[:, o:o + a]).astype(BF16)
    o += a
    k_ref[...] = _dot(xb, w_ref[:, o:o + a]).astype(BF16)
    o += a
    v_ref[...] = _dot(xb, w_ref[:, o:o + a]).astype(BF16)
    o += a
    sgc_ref[...] = _sigmoid(_dot(xb, w_ref[:, o:o + d])).astype(BF16)
    o += d
    sga_ref[...] = _sigmoid(_dot(xb, w_ref[:, o:o + d])).astype(BF16)


def _in_proj(x, w, *, c, a, tm):
    t, d = x.shape
    n = w.shape[1]
    row = lambda width: pl.BlockSpec((tm, width), lambda i: (i, 0))
    return pl.pallas_call(
        functools.partial(_in_proj_kernel, c=c, a=a, d=d),
        grid=(t // tm,),
        in_specs=[row(d), pl.BlockSpec((d, n), lambda i: (0, 0))],
        out_specs=[row(c), row(a), row(a), row(a), row(d), row(d)],
        out_shape=[jax.ShapeDtypeStruct((t, c), BF16)] + [jax.ShapeDtypeStruct((t, a), BF16)] * 3
        + [jax.ShapeDtypeStruct((t, d), BF16)] * 2,
        compiler_params=_params("arbitrary"),
        name="in_proj",
    )(x, w)


LOG2E = math.log2(math.e)
MASKED = float(jnp.finfo(jnp.bfloat16).min)
VT_ROWS = HEAD_DIM + BF16_SUBLANES


def _attn_kernel(q_ref, k_ref, v_ref, bias_ref, o_ref, vt_scr, kaug_scr, kmean_scr, sel_scr, sa_scr, sb_scr,
                 *, nb, nbp):
    blk = MOBA_BLOCK
    i = pl.program_id(2)

    @pl.when(i == 0)
    def _():
        if nbp > nb:
            kmean_scr[...] = jnp.zeros_like(kmean_scr)
        col = lax.broadcasted_iota(jnp.int32, (blk, LANES), 1)
        ones = jnp.ones((BF16_SUBLANES, blk), BF16)
        for j in range(nb):
            kb = k_ref[j * blk:(j + 1) * blk, :]
            kmean_scr[j:j + 1, :] = jnp.sum(kb.astype(F32), axis=0, keepdims=True) * (1.0 / blk)
            kaug_scr[j * blk:(j + 1) * blk, 0:LANES] = kb
            kaug_scr[j * blk:(j + 1) * blk, LANES:2 * LANES] = jnp.where(col == j, 1.0, 0.0).astype(BF16)
            vt = v_ref[j * blk:(j + 1) * blk, :].astype(F32).T.astype(BF16)
            for hh in range(2):
                vt_scr[j, hh, 0:HEAD_DIM, :] = vt[hh * HEAD_DIM:(hh + 1) * HEAD_DIM, :]
                vt_scr[j, hh, HEAD_DIM:VT_ROWS, :] = ones

    qt = q_ref[...].astype(F32).T * (HEAD_DIM ** -0.5 * LOG2E)
    qrow = lax.broadcasted_iota(jnp.int32, qt.shape, 0)
    kmean = kmean_scr[...].astype(BF16)
    brow = lax.broadcasted_iota(jnp.int32, (nbp, blk), 0)
    n_far = i - 1
    pad_rows = jnp.zeros((LANES - nbp, blk), BF16)

    qhs, qaugs = [], []
    for hh in range(2):
        lo = hh * HEAD_DIM
        qh = jnp.where((qrow >= lo) & (qrow < lo + HEAD_DIM), qt, 0.0).astype(BF16)
        qhs.append(qh)
        score = _dot(kmean, qh)
        cur = jnp.where(brow < i, score, NEG_INF)
        sel = jnp.zeros((nbp, blk), dtype=jnp.bool_)
        for t in range(MOBA_TOPK):
            mx = jnp.max(cur, axis=0, keepdims=True)
            first = jnp.min(jnp.where(cur == mx, brow, nbp), axis=0, keepdims=True)
            pick = brow == first
            sel = sel | (pick & (t < i))
            cur = jnp.where(pick, NEG_INF, cur)
        sel_scr[hh] = jnp.where(sel, 0.0, NEG_INF)
        far_rows = jnp.where(sel & (brow < n_far), 0.0, MASKED).astype(BF16)
        qaugs.append(jnp.concatenate([qh, far_rows, pad_rows], axis=0))

    def k_block(j):
        return k_ref[pl.ds(pl.multiple_of(j * blk, blk), blk), :]

    def sel_row(j, hh):
        return sel_scr[hh, pl.ds(j, 1), :]

    jp = jnp.maximum(i - 1, 0)
    n_units = (n_far + 1) // 2
    u_last = nb // 2 - 1

    def stage_ab_near(s_buf):
        k_prev, k_own = k_block(jp), k_block(i)
        cms = []
        for hh in range(2):
            s0 = _dot(k_prev, qhs[hh]) + bias_ref[0, hh, 1] + sel_row(jp, hh)
            s1 = _dot(k_own, qhs[hh]) + bias_ref[0, hh, 0]
            s_buf[hh, 0:blk, :] = s0
            s_buf[hh, blk:2 * blk, :] = s1
            cms.append(jnp.maximum(jnp.max(s0, axis=0, keepdims=True), jnp.max(s1, axis=0, keepdims=True)))
        return cms

    def stage_ab(u, s_buf):
        u = jnp.minimum(u, u_last)
        kaug = kaug_scr[pl.ds(pl.multiple_of(u * (2 * blk), 2 * blk), 2 * blk), :]
        cms = []
        for hh in range(2):
            s0 = _dot(kaug[0:blk], qaugs[hh])
            s1 = _dot(kaug[blk:2 * blk], qaugs[hh])
            s_buf[hh, 0:blk, :] = s0
            s_buf[hh, blk:2 * blk, :] = s1
            cms.append(jnp.maximum(jnp.max(s0, axis=0, keepdims=True), jnp.max(s1, axis=0, keepdims=True)))
        return cms

    def stage_c(j0, j1, s_buf, cms, carry):
        out = []
        for hh in range(2):
            m_old, acc_old = carry[2 * hh:2 * hh + 2]
            m_new = jnp.maximum(m_old, cms[hh])
            p0 = jnp.exp2(s_buf[hh, 0:blk, :] - m_new).astype(BF16)
            p1 = jnp.exp2(s_buf[hh, blk:2 * blk, :] - m_new).astype(BF16)
            acc_new = (jnp.exp2(m_old - m_new) * acc_old + _dot(vt_scr[j0, hh], p0) + _dot(vt_scr[j1, hh], p1))
            out += [m_new, acc_new]
        return out

    def stage_c_far(u, s_buf, cms, carry):
        u = jnp.minimum(u, u_last)
        return stage_c(2 * u, 2 * u + 1, s_buf, cms, carry)

    cm_near = stage_ab_near(sa_scr)
    cm_b = stage_ab(0, sb_scr)
    carry = []
    for hh in range(2):
        carry += [jnp.full((1, blk), NEG_INF, F32), jnp.zeros((VT_ROWS, blk), F32)]
    carry = stage_c(jp, i, sa_scr, cm_near, carry)

    def far_body(it, state):
        cm_b, carry = list(state[0:2]), list(state[2:])
        u0 = 2 * it
        cm_a = stage_ab(u0 + 1, sa_scr)
        carry = stage_c_far(u0, sb_scr, cm_b, carry)
        cm_b = stage_ab(u0 + 2, sb_scr)
        carry = stage_c_far(u0 + 1, sa_scr, cm_a, carry)
        return tuple(cm_b + carry)

    state = lax.fori_loop(0, (n_units + 1) // 2, far_body, tuple(cm_b + carry))
    carry = state[2:]
    outs = [carry[2 * hh + 1][0:HEAD_DIM] / carry[2 * hh + 1][HEAD_DIM:HEAD_DIM + 1] for hh in range(2)]
    o_ref[...] = jnp.concatenate(outs, axis=0).T.astype(BF16)


def _attention(q, k, v, bias_tables, *, bsz, s):
    t, a = q.shape
    blk = MOBA_BLOCK
    nb = s // blk
    nbp = -(-nb // BF16_SUBLANES) * BF16_SUBLANES
    assert nbp <= LANES and nb % 2 == 0
    npairs = a // LANES
    return pl.pallas_call(
        functools.partial(_attn_kernel, nb=nb, nbp=nbp),
        grid=(bsz, npairs, nb),
        in_specs=[
            pl.BlockSpec((blk, LANES), lambda b, p, i: (b * nb + i, p)),
            pl.BlockSpec((s, LANES), lambda b, p, i: (b, p)),
            pl.BlockSpec((s, LANES), lambda b, p, i: (b, p)),
            pl.BlockSpec((1, 2, 2, blk, blk), lambda b, p, i: (p, 0, 0, 0, 0)),
        ],
        out_specs=pl.BlockSpec((blk, LANES), lambda b, p, i: (b * nb + i, p)),
        out_shape=jax.ShapeDtypeStruct((t, a), BF16),
        scratch_shapes=[
            pltpu.VMEM((nb, 2, VT_ROWS, blk), BF16),
            pltpu.VMEM((nb * blk, 2 * LANES), BF16),
            pltpu.VMEM((nbp, LANES), F32),
            pltpu.VMEM((2, nbp, blk), F32),
            pltpu.VMEM((2, 2 * blk, blk), F32),
            pltpu.VMEM((2, 2 * blk, blk), F32),
        ],
        compiler_params=_params("arbitrary", "arbitrary", "arbitrary"),
        name="moba_attention",
    )(q, k, v, bias_tables)


def _bias_tables(rel_bias):
    blk = MOBA_BLOCK
    n = np.float32(blk + 1)
    half = REL_BUCKETS // 2
    far_bucket = half + int(np.log(n / np.float32(half)) / math.log(REL_MAX_DIST / half) * (REL_BUCKETS - half))
    assert far_bucket >= REL_BUCKETS - 1
    bias_t = rel_bias.T.astype(F32)
    bias_t = (bias_t - bias_t[:, REL_BUCKETS - 1:]) * LOG2E
    kk = jnp.arange(blk, dtype=jnp.int32)[:, None]
    qq = jnp.arange(blk, dtype=jnp.int32)[None, :]

    def lookup(bucket):
        out = jnp.zeros((bias_t.shape[0],) + bucket.shape, F32)
        for b in range(REL_BUCKETS):
            out = jnp.where(bucket[None] == b, bias_t[:, b][:, None, None], out)
        return out

    d_own = qq - kk
    own = jnp.where(d_own >= 0, lookup(_rel_bucket(d_own)), NEG_INF)
    prev = lookup(_rel_bucket(qq + blk - kk))
    tables = jnp.stack([own, prev], axis=1)
    return tables.reshape(N_HEADS // 2, 2, 2, blk, blk)


def _mix_kernel(x_ref, h_ref, halo_ref, o_ref, sgc_ref, sga_ref, cw_ref, cb_ref, cg_ref, cbeta_ref,
                wcp_ref, wap_ref, wout_ref, g1_ref, b1_ref, *rest, tm, tiles_per_seq, alpha, n_experts):
    if n_experts:
        rw_ref, x1_ref, x1r_ref, route_ref, hc_scr, sh_scr = rest
    else:
        x1_ref, hc_scr, sh_scr = rest
    i = pl.program_id(0)
    first = (i % tiles_per_seq) == 0
    hc_scr[0:HALO_ROWS, :] = jnp.where(first, 0.0, halo_ref[...].astype(F32))
    hc_scr[HALO_ROWS:, :] = h_ref[...].astype(F32)
    span = sh_scr.shape[1]
    for ph in range(1, SUBLANES):
        sh_scr[ph - 1] = hc_scr[ph:ph + span, :]

    def tap(kk):
        off = HALO_ROWS - (CONV_KERNEL - 1) + kk
        ph, start = off % SUBLANES, off - off % SUBLANES
        rows = hc_scr[start:start + tm, :] if ph == 0 else sh_scr[ph - 1, start:start + tm, :]
        return cw_ref[kk:kk + 1, :] * rows

    acc = cb_ref[...] + tap(0)
    for kk in range(1, CONV_KERNEL):
        acc = acc + tap(kk)
    hn = _layer_norm(acc, cg_ref[...], cbeta_ref[...])
    hs = (hn * _sigmoid(hn)).astype(BF16)
    y_conv = _dot(hs, wcp_ref[...])
    y_attn = _dot(o_ref[...], wap_ref[...])
    mixed = sgc_ref[...].astype(F32) * y_conv + sga_ref[...].astype(F32) * y_attn
    z = alpha * x_ref[...] + _dot(mixed.astype(BF16), wout_ref[...])
    x1 = _layer_norm(z, g1_ref[...], b1_ref[...])
    x1_ref[...] = x1
    if n_experts:
        nchunk = x1.shape[1] // LANES
        for cc in range(nchunk):
            x1r_ref[pl.ds(cc, tm, stride=nchunk), :] = x1[:, cc * LANES:(cc + 1) * LANES]
        logits = _dot(x1.astype(BF16), rw_ref[...])
        lane = lax.broadcasted_iota(jnp.int32, logits.shape, 1)
        lg = jnp.where(lane < n_experts, logits, NEG_INF)
        v1 = jnp.max(lg, axis=-1, keepdims=True)
        i1 = jnp.min(jnp.where(lg == v1, lane, LANES), axis=-1, keepdims=True)
        lg2 = jnp.where(lane == i1, NEG_INF, lg)
        v2 = jnp.max(lg2, axis=-1, keepdims=True)
        i2 = jnp.min(jnp.where(lg2 == v2, lane, LANES), axis=-1, keepdims=True)
        e = jnp.exp(v2 - v1)
        route_ref[...] = (jnp.where(lane == 0, i1.astype(F32), 0.0) + jnp.where(lane == 1, i2.astype(F32), 0.0)
                          + jnp.where(lane == 2, 1.0 / (1.0 + e), 0.0) + jnp.where(lane == 3, e / (1.0 + e), 0.0))


def _mix(x, h, o, sgc, sga, cw, cb, cg, cbeta, wcp, wap, wout, g1, b1, rw, *, s, tm, alpha, n_experts):
    t, d = x.shape
    c = h.shape[1]
    a = o.shape[1]
    row = lambda width: pl.BlockSpec((tm, width), lambda i: (i, 0))
    full = lambda arr: pl.BlockSpec(arr.shape, lambda i: (0,) * arr.ndim)
    halo_blocks = tm // HALO_ROWS
    in_specs = [row(d), row(c),
                pl.BlockSpec((HALO_ROWS, c), lambda i: (jnp.maximum(i * halo_blocks - 1, 0), 0)),
                row(a), row(d), row(d),
                full(cw), full(cb), full(cg), full(cbeta), full(wcp), full(wap), full(wout), full(g1), full(b1)]
    args = [x, h, h, o, sgc, sga, cw, cb, cg, cbeta, wcp, wap, wout, g1, b1]
    out_specs = [row(d)]
    out_shape = [jax.ShapeDtypeStruct((t, d), F32)]
    if n_experts:
        assert d == SUBLANES * LANES
        in_specs.append(full(rw))
        args.append(rw)
        out_specs += [pl.BlockSpec((tm * SUBLANES, LANES), lambda i: (i, 0)), row(LANES)]
        out_shape += [jax.ShapeDtypeStruct((t * SUBLANES, LANES), F32), jax.ShapeDtypeStruct((t, LANES), F32)]
    res = pl.pallas_call(
        functools.partial(_mix_kernel, tm=tm, tiles_per_seq=s // tm, alpha=alpha, n_experts=n_experts),
        grid=(t // tm,),
        in_specs=in_specs,
        out_specs=out_specs,
        out_shape=out_shape,
        scratch_shapes=[pltpu.VMEM((tm + HALO_ROWS, c), F32),
                        pltpu.VMEM((SUBLANES - 1, tm + HALO_ROWS - SUBLANES, c), F32)],
        compiler_params=_params("arbitrary"),
        name="mix",
    )(*args)
    return res if n_experts else (res[0], None, None)


def _ple(xb, p_ref, wpg_ref, wpp_ref):
    return _sigmoid(_dot(xb, wpg_ref[...])) * _dot(p_ref[...].astype(BF16), wpp_ref[...])


def _ffn_kernel(x1_ref, p_ref, wg_ref, wu_ref, wd_ref, wpg_ref, wpp_ref, g2_ref, b2_ref, out_ref,
                xb_scr, acc_scr, *, alpha):
    j = pl.program_id(1)

    @pl.when(j == 0)
    def _():
        xb_scr[...] = x1_ref[...].astype(BF16)
        acc_scr[...] = jnp.zeros_like(acc_scr)

    xb = xb_scr[...]
    g = _dot(xb, wg_ref[...])
    u = _dot(xb, wu_ref[...])
    acc_scr[...] += _dot((g * _sigmoid(g) * u).astype(BF16), wd_ref[...])

    @pl.when(j == pl.num_programs(1) - 1)
    def _():
        ple = _ple(xb, p_ref, wpg_ref, wpp_ref)
        out_ref[...] = _layer_norm(alpha * x1_ref[...] + acc_scr[...] + ple, g2_ref[...], b2_ref[...])


def _ffn(x1, p, wg, wu, wd, wpg, wpp, g2, b2, *, tm, fc, alpha):
    t, d = x1.shape
    f = wg.shape[1]
    row = lambda width: pl.BlockSpec((tm, width), lambda i, j: (i, 0))
    full = lambda arr: pl.BlockSpec(arr.shape, lambda i, j: (0,) * arr.ndim)
    return pl.pallas_call(
        functools.partial(_ffn_kernel, alpha=alpha),
        grid=(t // tm, f // fc),
        in_specs=[row(d), row(p.shape[1]),
                  pl.BlockSpec((d, fc), lambda i, j: (0, j)),
                  pl.BlockSpec((d, fc), lambda i, j: (0, j)),
                  pl.BlockSpec((fc, d), lambda i, j: (j, 0)),
                  full(wpg), full(wpp), full(g2), full(b2)],
        out_specs=row(d),
        out_shape=jax.ShapeDtypeStruct((t, d), F32),
        scratch_shapes=[pltpu.VMEM((tm, d), BF16), pltpu.VMEM((tm, d), F32)],
        compiler_params=_params("arbitrary", "arbitrary"),
        name="ffn",
    )(x1, p, wg, wu, wd, wpg, wpp, g2, b2)


DMA_UNROLL = 8


def _moe_plan(route, n_experts, tm):
    e_slot = route[:, 0:TOP_K].astype(jnp.int32).reshape(-1)
    n_slots = e_slot.shape[0]
    ids = jnp.arange(n_experts, dtype=jnp.int32)
    onehot = (e_slot[:, None] == ids[None, :]).astype(jnp.int32)
    csum = jnp.cumsum(onehot, axis=0)
    cnt = csum[-1]
    padded = ((cnt + tm - 1) // tm) * tm
    off_end = jnp.cumsum(padded)
    off = off_end - padded
    pos = jnp.sum((csum - onehot + off[None, :]) * onehot, axis=1)
    n_tiles = n_slots // tm + n_experts
    tile_start = jnp.arange(n_tiles, dtype=jnp.int32) * tm
    tile_expert = jnp.minimum(jnp.sum((tile_start[:, None] >= off_end[None, :]).astype(jnp.int32), axis=1),
                              n_experts - 1)
    tile_valid = (tile_start < off_end[-1]).astype(jnp.int32)
    return tile_expert, tile_valid, pos, n_tiles


def _row_copy(src, src_row, dst, dst_row, sem):
    return pltpu.make_async_copy(
        src.at[pl.ds(pl.multiple_of(src_row * SUBLANES, SUBLANES), SUBLANES), :],
        dst.at[pl.ds(pl.multiple_of(dst_row * SUBLANES, SUBLANES), SUBLANES), :], sem)


def _moe_dispatch_kernel(pos_ref, x1r_ref, init_ref, xs_ref, sem, *, tm):
    del init_ref

    def copy(tok, kk):
        return _row_copy(x1r_ref, tok, xs_ref, pos_ref[0, 0, TOP_K * tok + kk], sem)

    def issue(tok, carry):
        for kk in range(TOP_K):
            copy(tok, kk).start()
        return carry

    lax.fori_loop(0, tm, issue, 0, unroll=DMA_UNROLL)

    def drain(tok, carry):
        for kk in range(TOP_K):
            copy(tok, kk).wait()
        return carry

    lax.fori_loop(0, tm, drain, 0, unroll=DMA_UNROLL)


def _moe_dispatch(x1r, pos, n_rows, *, tm):
    t = x1r.shape[0] // SUBLANES
    pos3 = pos.reshape(t // tm, 1, TOP_K * tm)
    init = jnp.zeros((n_rows * SUBLANES, LANES), F32)
    return pl.pallas_call(
        functools.partial(_moe_dispatch_kernel, tm=tm),
        grid=(t // tm,),
        in_specs=[pl.BlockSpec((1, 1, TOP_K * tm), lambda i: (i, 0, 0), memory_space=pltpu.SMEM),
                  pl.BlockSpec((tm * SUBLANES, LANES), lambda i: (i, 0)), pl.BlockSpec(memory_space=pl.ANY)],
        out_specs=pl.BlockSpec(memory_space=pl.ANY),
        out_shape=jax.ShapeDtypeStruct(init.shape, F32),
        scratch_shapes=[pltpu.SemaphoreType.DMA],
        input_output_aliases={2: 0},
        compiler_params=_params("arbitrary"),
        name="moe_dispatch",
    )(pos3, x1r, init)


def _moe_ffn_kernel(te_ref, valid_ref, xs_ref, wg_ref, wu_ref, wd_ref, y_ref, xb_scr, acc_scr, *, tm):
    i = pl.program_id(0)
    j = pl.program_id(1)
    valid = valid_ref[i] != 0
    nchunk = xb_scr.shape[1] // LANES

    @pl.when(valid & (j == 0))
    def _():
        for cc in range(nchunk):
            xb_scr[:, cc * LANES:(cc + 1) * LANES] = xs_ref[pl.ds(cc, tm, stride=nchunk), :].astype(BF16)
        acc_scr[...] = jnp.zeros_like(acc_scr)

    @pl.when(valid)
    def _():
        xb = xb_scr[...]
        g = _dot(xb, wg_ref[0])
        u = _dot(xb, wu_ref[0])
        acc_scr[...] += _dot((g * _sigmoid(g) * u).astype(BF16), wd_ref[0])

    last = j == pl.num_programs(1) - 1

    @pl.when(last & valid)
    def _():
        for cc in range(nchunk):
            y_ref[pl.ds(cc, tm, stride=nchunk), :] = acc_scr[:, cc * LANES:(cc + 1) * LANES]

    @pl.when(last & jnp.logical_not(valid))
    def _():
        y_ref[...] = jnp.zeros_like(y_ref)


def _moe_ffn(xs, tile_expert, tile_valid, wg, wu, wd, *, tm, fc):
    ne, d, f = wg.shape
    n_tiles = xs.shape[0] // (tm * SUBLANES)
    nj = f // fc
    jj = lambda i, j, te, tv: jnp.where(tv[i] != 0, j, nj - 1)
    rows = pl.BlockSpec((tm * SUBLANES, LANES), lambda i, j, te, tv: (i, 0))
    return pl.pallas_call(
        functools.partial(_moe_ffn_kernel, tm=tm),
        grid_spec=pltpu.PrefetchScalarGridSpec(
            num_scalar_prefetch=2,
            grid=(n_tiles, nj),
            in_specs=[
                rows,
                pl.BlockSpec((1, d, fc), lambda i, j, te, tv: (te[i], 0, jj(i, j, te, tv))),
                pl.BlockSpec((1, d, fc), lambda i, j, te, tv: (te[i], 0, jj(i, j, te, tv))),
                pl.BlockSpec((1, fc, d), lambda i, j, te, tv: (te[i], jj(i, j, te, tv), 0)),
            ],
            out_specs=rows,
            scratch_shapes=[pltpu.VMEM((tm, d), BF16), pltpu.VMEM((tm, d), F32)],
        ),
        out_shape=jax.ShapeDtypeStruct(xs.shape, F32),
        compiler_params=_params("arbitrary", "arbitrary"),
        name="moe_ffn",
    )(tile_expert, tile_valid, xs, wg, wu, wd)


def _moe_combine_kernel(pos_ref, x1_ref, p_ref, route_ref, yr_ref, wpg_ref, wpp_ref, g2_ref, b2_ref, out_ref,
                        yg_scr, sem, *, tm, alpha):
    d = x1_ref.shape[1]
    nchunk = d // LANES

    def copy(tok, kk):
        return _row_copy(yr_ref, pos_ref[0, 0, TOP_K * tok + kk], yg_scr, kk * tm + tok, sem)

    def issue(tok, carry):
        for kk in range(TOP_K):
            copy(tok, kk).start()
        return carry

    lax.fori_loop(0, tm, issue, 0, unroll=DMA_UNROLL)
    x1 = x1_ref[...]
    ple = _ple(x1.astype(BF16), p_ref, wpg_ref, wpp_ref)

    def drain(tok, carry):
        for kk in range(TOP_K):
            copy(tok, kk).wait()
        return carry

    lax.fori_loop(0, tm, drain, 0, unroll=DMA_UNROLL)
    route = route_ref[...]
    gates = [route[:, TOP_K + kk:TOP_K + kk + 1] for kk in range(TOP_K)]
    chunks = []
    for cc in range(nchunk):
        f = gates[0] * yg_scr[pl.ds(cc, tm, stride=nchunk), :]
        for kk in range(1, TOP_K):
            f = f + gates[kk] * yg_scr[pl.ds(kk * tm * nchunk + cc, tm, stride=nchunk), :]
        chunks.append(f)
    ffn = jnp.concatenate(chunks, axis=1)
    out_ref[...] = _layer_norm(alpha * x1 + ffn + ple, g2_ref[...], b2_ref[...])


def _moe_combine(x1, p, route, yr, pos, wpg, wpp, g2, b2, *, tm, alpha):
    t, d = x1.shape
    row = lambda width: pl.BlockSpec((tm, width), lambda i: (i, 0))
    full = lambda arr: pl.BlockSpec(arr.shape, lambda i: (0,) * arr.ndim)
    pos3 = pos.reshape(t // tm, 1, TOP_K * tm)
    return pl.pallas_call(
        functools.partial(_moe_combine_kernel, tm=tm, alpha=alpha),
        grid=(t // tm,),
        in_specs=[pl.BlockSpec((1, 1, TOP_K * tm), lambda i: (i, 0, 0), memory_space=pltpu.SMEM),
                  row(d), row(p.shape[1]), row(LANES), pl.BlockSpec(memory_space=pl.ANY),
                  full(wpg), full(wpp), full(g2), full(b2)],
        out_specs=row(d),
        out_shape=jax.ShapeDtypeStruct((t, d), F32),
        scratch_shapes=[pltpu.VMEM((TOP_K * tm * SUBLANES, LANES), F32), pltpu.SemaphoreType.DMA],
        compiler_params=_params("arbitrary"),
        name="moe_combine",
    )(pos3, x1, p, route, yr, wpg, wpp, g2, b2)


def kernel(x, p, rel_bias, w_in, conv_w, conv_b, conv_ln_g, conv_ln_b, w_conv_proj, w_attn_proj, w_out, ln1_g, ln1_b, w_ple_gate, w_ple_proj, ln2_g, ln2_b, ffn_w_gate, ffn_w_up, ffn_w_down, router_w, exp_w_gate, exp_w_up, exp_w_down):
    bsz, s, d = x.shape
    depth = w_in.shape[0]
    c = conv_w.shape[2]
    a = w_attn_proj.shape[1]
    n_experts = router_w.shape[2]
    assert a == N_HEADS * HEAD_DIM and s % (2 * MOBA_BLOCK) == 0 and conv_w.shape[1] == CONV_KERNEL
    assert n_experts >= TOP_K
    alpha = (2 * depth) ** 0.25
    t = bsz * s
    tm = 512
    tm_combine = 256
    fc = 512
    assert s % tm == 0 and (TOP_K * t) % tm == 0

    tables = _bias_tables(rel_bias)
    xf = x.reshape(t, d)
    row2 = lambda v: v.reshape(1, -1).astype(F32)
    for i in range(depth):
        h, q, k, v, sgc, sga = _in_proj(xf, w_in[i].astype(BF16), c=c, a=a, tm=tm)
        o = _attention(q, k, v, tables, bsz=bsz, s=s)
        moe = i % 2 == 1
        cw = jnp.pad(conv_w[i], ((0, HALO_ROWS - CONV_KERNEL), (0, 0)))
        rw = jnp.pad(router_w[i // 2], ((0, 0), (0, LANES - n_experts))).astype(BF16) if moe else None
        x1, x1r, route = _mix(xf, h, o, sgc, sga, cw, row2(conv_b[i]), row2(conv_ln_g[i]), row2(conv_ln_b[i]),
                              w_conv_proj[i].astype(BF16), w_attn_proj[i].astype(BF16), w_out[i].astype(BF16),
                              row2(ln1_g[i]), row2(ln1_b[i]), rw,
                              s=s, tm=tm, alpha=alpha, n_experts=n_experts if moe else 0)
        tail = (w_ple_gate[i].astype(BF16), w_ple_proj[i].astype(BF16), row2(ln2_g[i]), row2(ln2_b[i]))
        p_i = p[i].reshape(t, -1)
        if moe:
            tile_expert, tile_valid, pos, n_tiles = _moe_plan(route, n_experts, tm)
            xs = _moe_dispatch(x1r, pos, n_tiles * tm, tm=tm)
            yr = _moe_ffn(xs, tile_expert, tile_valid, exp_w_gate[i // 2].astype(BF16),
                          exp_w_up[i // 2].astype(BF16), exp_w_down[i // 2].astype(BF16), tm=tm, fc=fc)
            xf = _moe_combine(x1, p_i, route, yr, pos, *tail, tm=tm_combine, alpha=alpha)
        else:
            xf = _ffn(x1, p_i, ffn_w_gate[i // 2].astype(BF16), ffn_w_up[i // 2].astype(BF16),
                      ffn_w_down[i // 2].astype(BF16), *tail, tm=tm, fc=fc, alpha=alpha)
    return xf.reshape(bsz, s, d)
```

```python
import functools
import math

import numpy as np
import jax
import jax.numpy as jnp
from jax import lax
from jax.experimental import pallas as pl
from jax.experimental.pallas import tpu as pltpu

N_HEADS = 8
HEAD_DIM = 64
CONV_KERNEL = 31
MOBA_BLOCK = 256
MOBA_TOPK = 3
REL_BUCKETS = 32
REL_MAX_DIST = 128
TOP_K = 2
LN_EPS = 1e-5

LANES = 128
SUBLANES = 8
BF16_SUBLANES = 16
HALO_ROWS = 32
VMEM_LIMIT = 56 * 1024 * 1024

F32 = jnp.float32
BF16 = jnp.bfloat16
NEG_INF = float("-inf")


def _sigmoid(t):
    return 1.0 / (1.0 + jnp.exp(-t))


def _layer_norm(t, g, b):
    mu = jnp.mean(t, axis=-1, keepdims=True)
    d = t - mu
    var = jnp.mean(d * d, axis=-1, keepdims=True)
    return d * lax.rsqrt(var + LN_EPS) * g + b


def _dot(a, b):
    return jnp.dot(a, b, preferred_element_type=F32)


def _params(*semantics):
    return pltpu.CompilerParams(dimension_semantics=semantics, vmem_limit_bytes=VMEM_LIMIT)


def _rel_bucket(dist):
    n = jnp.maximum(dist, 0)
    max_exact = REL_BUCKETS // 2
    nf = jnp.maximum(n, 1).astype(F32)
    large = max_exact + (jnp.log(nf / max_exact) / math.log(REL_MAX_DIST / max_exact)
                         * (REL_BUCKETS - max_exact)).astype(jnp.int32)
    large = jnp.minimum(large, REL_BUCKETS - 1)
    return jnp.where(n < max_exact, n, large)


def _in_proj_kernel(x_ref, w_ref, h_ref, q_ref, k_ref, v_ref, sgc_ref, sga_ref, *, c, a, d):
    xb = x_ref[...].astype(BF16)
    o = 0
    glu_in = _dot(xb, w_ref[:, o:o + c])
    glu_gate = _dot(xb, w_ref[:, o + c:o + 2 * c])
    h_ref[...] = (glu_in * _sigmoid(glu_gate)).astype(BF16)
    o += 2 * c
    q_ref[...] = _dot(xb, w_ref[:, o:o + a]).astype(BF16)
    o += a
    k_ref[...] = _dot(xb, w_ref[:, o:o + a]).astype(BF16)
    o += a
    v_ref[...] = _dot(xb, w_ref[:, o:o + a]).astype(BF16)
    o += a
    sgc_ref[...] = _sigmoid(_dot(xb, w_ref[:, o:o + d])).astype(BF16)
    o += d
    sga_ref[...] = _sigmoid(_dot(xb, w_ref[:, o:o + d])).astype(BF16)


def _in_proj(x, w, *, c, a, tm):
    t, d = x.shape
    n = w.shape[1]
    row = lambda width: pl.BlockSpec((tm, width), lambda i: (i, 0))
    return pl.pallas_call(
        functools.partial(_in_proj_kernel, c=c, a=a, d=d),
        grid=(t // tm,),
        in_specs=[row(d), pl.BlockSpec((d, n), lambda i: (0, 0))],
        out_specs=[row(c), row(a), row(a), row(a), row(d), row(d)],
        out_shape=[jax.ShapeDtypeStruct((t, c), BF16)] + [jax.ShapeDtypeStruct((t, a), BF16)] * 3
        + [jax.ShapeDtypeStruct((t, d), BF16)] * 2,
        compiler_params=_params("arbitrary"),
        name="in_proj",
    )(x, w)


LOG2E = math.log2(math.e)
MASKED = float(jnp.finfo(jnp.bfloat16).min)
VT_ROWS = HEAD_DIM + BF16_SUBLANES


def _attn_kernel(q_ref, k_ref, v_ref, bias_ref, o_ref, vt_scr, kaug_scr, kmean_scr, sel_scr, sa_scr, sb_scr,
                 *, nb, nbp):
    blk = MOBA_BLOCK
    i = pl.program_id(2)

    @pl.when(i == 0)
    def _():
        if nbp > nb:
            kmean_scr[...] = jnp.zeros_like(kmean_scr)
        col = lax.broadcasted_iota(jnp.int32, (blk, LANES), 1)
        ones = jnp.ones((BF16_SUBLANES, blk), BF16)
        for j in range(nb):
            kb = k_ref[j * blk:(j + 1) * blk, :]
            kmean_scr[j:j + 1, :] = jnp.sum(kb.astype(F32), axis=0, keepdims=True) * (1.0 / blk)
            kaug_scr[j * blk:(j + 1) * blk, 0:LANES] = kb
            kaug_scr[j * blk:(j + 1) * blk, LANES:2 * LANES] = jnp.where(col == j, 1.0, 0.0).astype(BF16)
            vt = v_ref[j * blk:(j + 1) * blk, :].astype(F32).T.astype(BF16)
            for hh in range(2):
                vt_scr[j, hh, 0:HEAD_DIM, :] = vt[hh * HEAD_DIM:(hh + 1) * HEAD_DIM, :]
                vt_scr[j, hh, HEAD_DIM:VT_ROWS, :] = ones

    qt = q_ref[...].astype(F32).T * (HEAD_DIM ** -0.5 * LOG2E)
    qrow = lax.broadcasted_iota(jnp.int32, qt.shape, 0)
    kmean = kmean_scr[...].astype(BF16)
    brow = lax.broadcasted_iota(jnp.int32, (nbp, blk), 0)
    n_far = i - 1
    pad_rows = jnp.zeros((LANES - nbp, blk), BF16)

    qhs, qaugs = [], []
    for hh in range(2):
        lo = hh * HEAD_DIM
        qh = jnp.where((qrow >= lo) & (qrow < lo + HEAD_DIM), qt, 0.0).astype(BF16)
        qhs.append(qh)
        score = _dot(kmean, qh)
        cur = jnp.where(brow < i, score, NEG_INF)
        sel = jnp.zeros((nbp, blk), dtype=jnp.bool_)
        for t in range(MOBA_TOPK):
            mx = jnp.max(cur, axis=0, keepdims=True)
            first = jnp.min(jnp.where(cur == mx, brow, nbp), axis=0, keepdims=True)
            pick = brow == first
            sel = sel | (pick & (t < i))
            cur = jnp.where(pick, NEG_INF, cur)
        sel_scr[hh] = jnp.where(sel, 0.0, NEG_INF)
        far_rows = jnp.where(sel & (brow < n_far), 0.0, MASKED).astype(BF16)
        qaugs.append(jnp.concatenate([qh, far_rows, pad_rows], axis=0))

    def k_block(j):
        return k_ref[pl.ds(pl.multiple_of(j * blk, blk), blk), :]

    def sel_row(j, hh):
        return sel_scr[hh, pl.ds(j, 1), :]

    jp = jnp.maximum(i - 1, 0)
    n_units = (n_far + 1) // 2
    u_last = nb // 2 - 1

    def stage_ab_near(s_buf):
        k_prev, k_own = k_block(jp), k_block(i)
        cms = []
        for hh in range(2):
            s0 = _dot(k_prev, qhs[hh]) + bias_ref[0, hh, 1] + sel_row(jp, hh)
            s1 = _dot(k_own, qhs[hh]) + bias_ref[0, hh, 0]
            s_buf[hh, 0:blk, :] = s0
            s_buf[hh, blk:2 * blk, :] = s1
            cms.append(jnp.maximum(jnp.max(s0, axis=0, keepdims=True), jnp.max(s1, axis=0, keepdims=True)))
        return cms

    def stage_ab(u, s_buf):
        u = jnp.minimum(u, u_last)
        kaug = kaug_scr[pl.ds(pl.multiple_of(u * (2 * blk), 2 * blk), 2 * blk), :]
        cms = []
        for hh in range(2):
            s0 = _dot(kaug[0:blk], qaugs[hh])
            s1 = _dot(kaug[blk:2 * blk], qaugs[hh])
            s_buf[hh, 0:blk, :] = s0
            s_buf[hh, blk:2 * blk, :] = s1
            cms.append(jnp.maximum(jnp.max(s0, axis=0, keepdims=True), jnp.max(s1, axis=0, keepdims=True)))
        return cms

    def stage_c(j0, j1, s_buf, cms, carry):
        out = []
        for hh in range(2):
            m_old, acc_old = carry[2 * hh:2 * hh + 2]
            m_new = jnp.maximum(m_old, cms[hh])
            p0 = jnp.exp2(s_buf[hh, 0:blk, :] - m_new).astype(BF16)
            p1 = jnp.exp2(s_buf[hh, blk:2 * blk, :] - m_new).astype(BF16)
            acc_new = (jnp.exp2(m_old - m_new) * acc_old + _dot(vt_scr[j0, hh], p0) + _dot(vt_scr[j1, hh], p1))
            out += [m_new, acc_new]
        return out

    def stage_c_far(u, s_buf, cms, carry):
        u = jnp.minimum(u, u_last)
        return stage_c(2 * u, 2 * u + 1, s_buf, cms, carry)

    cm_near = stage_ab_near(sa_scr)
    cm_b = stage_ab(0, sb_scr)
    carry = []
    for hh in range(2):
        carry += [jnp.full((1, blk), NEG_INF, F32), jnp.zeros((VT_ROWS, blk), F32)]
    carry = stage_c(jp, i, sa_scr, cm_near, carry)

    def far_body(it, state):
        cm_b, carry = list(state[0:2]), list(state[2:])
        u0 = 2 * it
        cm_a = stage_ab(u0 + 1, sa_scr)
        carry = stage_c_far(u0, sb_scr, cm_b, carry)
        cm_b = stage_ab(u0 + 2, sb_scr)
        carry = stage_c_far(u0 + 1, sa_scr, cm_a, carry)
        return tuple(cm_b + carry)

    state = lax.fori_loop(0, (n_units + 1) // 2, far_body, tuple(cm_b + carry))
    carry = state[2:]
    outs = [carry[2 * hh + 1][0:HEAD_DIM] / carry[2 * hh + 1][HEAD_DIM:HEAD_DIM + 1] for hh in range(2)]
    o_ref[...] = jnp.concatenate(outs, axis=0).T.astype(BF16)


def _attention(q, k, v, bias_tables, *, bsz, s):
    t, a = q.shape
    blk = MOBA_BLOCK
    nb = s // blk
    nbp = -(-nb // BF16_SUBLANES) * BF16_SUBLANES
    assert nbp <= LANES and nb % 2 == 0
    npairs = a // LANES
    return pl.pallas_call(
        functools.partial(_attn_kernel, nb=nb, nbp=nbp),
        grid=(bsz, npairs, nb),
        in_specs=[
            pl.BlockSpec((blk, LANES), lambda b, p, i: (b * nb + i, p)),
            pl.BlockSpec((s, LANES), lambda b, p, i: (b, p)),
            pl.BlockSpec((s, LANES), lambda b, p, i: (b, p)),
            pl.BlockSpec((1, 2, 2, blk, blk), lambda b, p, i: (p, 0, 0, 0, 0)),
        ],
        out_specs=pl.BlockSpec((blk, LANES), lambda b, p, i: (b * nb + i, p)),
        out_shape=jax.ShapeDtypeStruct((t, a), BF16),
        scratch_shapes=[
            pltpu.VMEM((nb, 2, VT_ROWS, blk), BF16),
            pltpu.VMEM((nb * blk, 2 * LANES), BF16),
            pltpu.VMEM((nbp, LANES), F32),
            pltpu.VMEM((2, nbp, blk), F32),
            pltpu.VMEM((2, 2 * blk, blk), F32),
            pltpu.VMEM((2, 2 * blk, blk), F32),
        ],
        compiler_params=_params("arbitrary", "arbitrary", "arbitrary"),
        name="moba_attention",
    )(q, k, v, bias_tables)


def _bias_tables(rel_bias):
    blk = MOBA_BLOCK
    n = np.float32(blk + 1)
    half = REL_BUCKETS // 2
    far_bucket = half + int(np.log(n / np.float32(half)) / math.log(REL_MAX_DIST / half) * (REL_BUCKETS - half))
    assert far_bucket >= REL_BUCKETS - 1
    bias_t = rel_bias.T.astype(F32)
    bias_t = (bias_t - bias_t[:, REL_BUCKETS - 1:]) * LOG2E
    kk = jnp.arange(blk, dtype=jnp.int32)[:, None]
    qq = jnp.arange(blk, dtype=jnp.int32)[None, :]

    def lookup(bucket):
        out = jnp.zeros((bias_t.shape[0],) + bucket.shape, F32)
        for b in range(REL_BUCKETS):
            out = jnp.where(bucket[None] == b, bias_t[:, b][:, None, None], out)
        return out

    d_own = qq - kk
    own = jnp.where(d_own >= 0, lookup(_rel_bucket(d_own)), NEG_INF)
    prev = lookup(_rel_bucket(qq + blk - kk))
    tables = jnp.stack([own, prev], axis=1)
    return tables.reshape(N_HEADS // 2, 2, 2, blk, blk)


def _mix_kernel(x_ref, h_ref, halo_ref, o_ref, sgc_ref, sga_ref, p_ref, cw_ref, cb_ref, cg_ref, cbeta_ref,
                wcp_ref, wap_ref, wout_ref, g1_ref, b1_ref, wpg_ref, wpp_ref, *rest,
                tm, tiles_per_seq, alpha, n_experts):
    if n_experts:
        rw_ref, x1_ref, ple_ref, x1r_ref, route_ref, hc_scr, sh_scr = rest
    else:
        x1_ref, ple_ref, hc_scr, sh_scr = rest
    i = pl.program_id(0)
    first = (i % tiles_per_seq) == 0
    hc_scr[0:HALO_ROWS, :] = jnp.where(first, 0.0, halo_ref[...].astype(F32))
    hc_scr[HALO_ROWS:, :] = h_ref[...].astype(F32)
    span = sh_scr.shape[1]
    for ph in range(1, SUBLANES):
        sh_scr[ph - 1] = hc_scr[ph:ph + span, :]

    def tap(kk):
        off = HALO_ROWS - (CONV_KERNEL - 1) + kk
        ph, start = off % SUBLANES, off - off % SUBLANES
        rows = hc_scr[start:start + tm, :] if ph == 0 else sh_scr[ph - 1, start:start + tm, :]
        return cw_ref[kk:kk + 1, :] * rows

    acc = cb_ref[...] + tap(0)
    for kk in range(1, CONV_KERNEL):
        acc = acc + tap(kk)
    hn = _layer_norm(acc, cg_ref[...], cbeta_ref[...])
    hs = (hn * _sigmoid(hn)).astype(BF16)
    y_conv = _dot(hs, wcp_ref[...])
    y_attn = _dot(o_ref[...], wap_ref[...])
    mixed = sgc_ref[...].astype(F32) * y_conv + sga_ref[...].astype(F32) * y_attn
    z = alpha * x_ref[...] + _dot(mixed.astype(BF16), wout_ref[...])
    x1 = _layer_norm(z, g1_ref[...], b1_ref[...])
    x1_ref[...] = x1
    x1b = x1.astype(BF16)
    ple = _sigmoid(_dot(x1b, wpg_ref[...])) * _dot(p_ref[...].astype(BF16), wpp_ref[...])
    ple_ref[...] = ple.astype(BF16)
    if n_experts:
        nchunk = x1.shape[1] // LANES
        for cc in range(nchunk):
            x1r_ref[pl.ds(cc, tm, stride=nchunk), :] = x1[:, cc * LANES:(cc + 1) * LANES]
        logits = _dot(x1b, rw_ref[...])
        lane = lax.broadcasted_iota(jnp.int32, logits.shape, 1)
        lg = jnp.where(lane < n_experts, logits, NEG_INF)
        v1 = jnp.max(lg, axis=-1, keepdims=True)
        i1 = jnp.min(jnp.where(lg == v1, lane, LANES), axis=-1, keepdims=True)
        lg2 = jnp.where(lane == i1, NEG_INF, lg)
        v2 = jnp.max(lg2, axis=-1, keepdims=True)
        i2 = jnp.min(jnp.where(lg2 == v2, lane, LANES), axis=-1, keepdims=True)
        e = jnp.exp(v2 - v1)
        route_ref[...] = (jnp.where(lane == 0, i1.astype(F32), 0.0) + jnp.where(lane == 1, i2.astype(F32), 0.0)
                          + jnp.where(lane == 2, 1.0 / (1.0 + e), 0.0) + jnp.where(lane == 3, e / (1.0 + e), 0.0))


def _mix(x, h, o, sgc, sga, p, cw, cb, cg, cbeta, wcp, wap, wout, g1, b1, wpg, wpp, rw,
         *, s, tm, alpha, n_experts):
    t, d = x.shape
    c = h.shape[1]
    a = o.shape[1]
    row = lambda width: pl.BlockSpec((tm, width), lambda i: (i, 0))
    full = lambda arr: pl.BlockSpec(arr.shape, lambda i: (0,) * arr.ndim)
    halo_blocks = tm // HALO_ROWS
    in_specs = [row(d), row(c),
                pl.BlockSpec((HALO_ROWS, c), lambda i: (jnp.maximum(i * halo_blocks - 1, 0), 0)),
                row(a), row(d), row(d), row(p.shape[1]),
                full(cw), full(cb), full(cg), full(cbeta), full(wcp), full(wap), full(wout), full(g1), full(b1),
                full(wpg), full(wpp)]
    args = [x, h, h, o, sgc, sga, p, cw, cb, cg, cbeta, wcp, wap, wout, g1, b1, wpg, wpp]
    out_specs = [row(d), row(d)]
    out_shape = [jax.ShapeDtypeStruct((t, d), F32), jax.ShapeDtypeStruct((t, d), BF16)]
    if n_experts:
        assert d == SUBLANES * LANES
        in_specs.append(full(rw))
        args.append(rw)
        out_specs += [pl.BlockSpec((tm * SUBLANES, LANES), lambda i: (i, 0)), row(LANES)]
        out_shape += [jax.ShapeDtypeStruct((t * SUBLANES, LANES), F32), jax.ShapeDtypeStruct((t, LANES), F32)]
    res = pl.pallas_call(
        functools.partial(_mix_kernel, tm=tm, tiles_per_seq=s // tm, alpha=alpha, n_experts=n_experts),
        grid=(t // tm,),
        in_specs=in_specs,
        out_specs=out_specs,
        out_shape=out_shape,
        scratch_shapes=[pltpu.VMEM((tm + HALO_ROWS, c), F32),
                        pltpu.VMEM((SUBLANES - 1, tm + HALO_ROWS - SUBLANES, c), F32)],
        compiler_params=_params("arbitrary"),
        name="mix",
    )(*args)
    return res if n_experts else (res[0], res[1], None, None)


def _ffn_kernel(x1_ref, ple_ref, wg_ref, wu_ref, wd_ref, g2_ref, b2_ref, out_ref, xb_scr, acc_scr, *, alpha):
    j = pl.program_id(1)

    @pl.when(j == 0)
    def _():
        xb_scr[...] = x1_ref[...].astype(BF16)
        acc_scr[...] = jnp.zeros_like(acc_scr)

    xb = xb_scr[...]
    g = _dot(xb, wg_ref[...])
    u = _dot(xb, wu_ref[...])
    acc_scr[...] += _dot((g * _sigmoid(g) * u).astype(BF16), wd_ref[...])

    @pl.when(j == pl.num_programs(1) - 1)
    def _():
        out_ref[...] = _layer_norm(alpha * x1_ref[...] + acc_scr[...] + ple_ref[...].astype(F32),
                                   g2_ref[...], b2_ref[...])


def _ffn(x1, ple, wg, wu, wd, g2, b2, *, tm, fc, alpha):
    t, d = x1.shape
    f = wg.shape[1]
    row = lambda width: pl.BlockSpec((tm, width), lambda i, j: (i, 0))
    full = lambda arr: pl.BlockSpec(arr.shape, lambda i, j: (0,) * arr.ndim)
    return pl.pallas_call(
        functools.partial(_ffn_kernel, alpha=alpha),
        grid=(t // tm, f // fc),
        in_specs=[row(d), row(d),
                  pl.BlockSpec((d, fc), lambda i, j: (0, j)),
                  pl.BlockSpec((d, fc), lambda i, j: (0, j)),
                  pl.BlockSpec((fc, d), lambda i, j: (j, 0)),
                  full(g2), full(b2)],
        out_specs=row(d),
        out_shape=jax.ShapeDtypeStruct((t, d), F32),
        scratch_shapes=[pltpu.VMEM((tm, d), BF16), pltpu.VMEM((tm, d), F32)],
        compiler_params=_params("arbitrary", "arbitrary"),
        name="ffn",
    )(x1, ple, wg, wu, wd, g2, b2)


DMA_UNROLL = 8


def _moe_plan(route, n_experts, tm):
    e_slot = route[:, 0:TOP_K].astype(jnp.int32).reshape(-1)
    n_slots = e_slot.shape[0]
    ids = jnp.arange(n_experts, dtype=jnp.int32)
    onehot = (e_slot[:, None] == ids[None, :]).astype(jnp.int32)
    csum = jnp.cumsum(onehot, axis=0)
    cnt = csum[-1]
    padded = ((cnt + tm - 1) // tm) * tm
    off_end = jnp.cumsum(padded)
    off = off_end - padded
    pos = jnp.sum((csum - onehot + off[None, :]) * onehot, axis=1)
    n_tiles = n_slots // tm + n_experts
    tile_start = jnp.arange(n_tiles, dtype=jnp.int32) * tm
    tile_expert = jnp.minimum(jnp.sum((tile_start[:, None] >= off_end[None, :]).astype(jnp.int32), axis=1),
                              n_experts - 1)
    tile_valid = (tile_start < off_end[-1]).astype(jnp.int32)
    return tile_expert, tile_valid, pos, n_tiles


def _row_copy(src, src_row, dst, dst_row, sem):
    return pltpu.make_async_copy(
        src.at[pl.ds(pl.multiple_of(src_row * SUBLANES, SUBLANES), SUBLANES), :],
        dst.at[pl.ds(pl.multiple_of(dst_row * SUBLANES, SUBLANES), SUBLANES), :], sem)


def _moe_dispatch_kernel(pos_ref, x1r_ref, init_ref, xs_ref, sem, *, tm):
    del init_ref

    def copy(tok, kk):
        return _row_copy(x1r_ref, tok, xs_ref, pos_ref[0, 0, TOP_K * tok + kk], sem)

    def issue(tok, carry):
        for kk in range(TOP_K):
            copy(tok, kk).start()
        return carry

    lax.fori_loop(0, tm, issue, 0, unroll=DMA_UNROLL)

    def drain(tok, carry):
        for kk in range(TOP_K):
            copy(tok, kk).wait()
        return carry

    lax.fori_loop(0, tm, drain, 0, unroll=DMA_UNROLL)


def _moe_dispatch(x1r, pos, n_rows, *, tm):
    t = x1r.shape[0] // SUBLANES
    pos3 = pos.reshape(t // tm, 1, TOP_K * tm)
    init = jnp.zeros((n_rows * SUBLANES, LANES), F32)
    return pl.pallas_call(
        functools.partial(_moe_dispatch_kernel, tm=tm),
        grid=(t // tm,),
        in_specs=[pl.BlockSpec((1, 1, TOP_K * tm), lambda i: (i, 0, 0), memory_space=pltpu.SMEM),
                  pl.BlockSpec((tm * SUBLANES, LANES), lambda i: (i, 0)), pl.BlockSpec(memory_space=pl.ANY)],
        out_specs=pl.BlockSpec(memory_space=pl.ANY),
        out_shape=jax.ShapeDtypeStruct(init.shape, F32),
        scratch_shapes=[pltpu.SemaphoreType.DMA],
        input_output_aliases={2: 0},
        compiler_params=_params("arbitrary"),
        name="moe_dispatch",
    )(pos3, x1r, init)


def _moe_ffn_kernel(te_ref, valid_ref, xs_ref, wg_ref, wu_ref, wd_ref, y_ref, xb_scr, acc_scr, *, tm):
    i = pl.program_id(0)
    j = pl.program_id(1)
    valid = valid_ref[i] != 0
    nchunk = xb_scr.shape[1] // LANES

    @pl.when(valid & (j == 0))
    def _():
        for cc in range(nchunk):
            xb_scr[:, cc * LANES:(cc + 1) * LANES] = xs_ref[pl.ds(cc, tm, stride=nchunk), :].astype(BF16)
        acc_scr[...] = jnp.zeros_like(acc_scr)

    @pl.when(valid)
    def _():
        xb = xb_scr[...]
        g = _dot(xb, wg_ref[0])
        u = _dot(xb, wu_ref[0])
        acc_scr[...] += _dot((g * _sigmoid(g) * u).astype(BF16), wd_ref[0])

    last = j == pl.num_programs(1) - 1

    @pl.when(last & valid)
    def _():
        for cc in range(nchunk):
            y_ref[pl.ds(cc, tm, stride=nchunk), :] = acc_scr[:, cc * LANES:(cc + 1) * LANES]

    @pl.when(last & jnp.logical_not(valid))
    def _():
        y_ref[...] = jnp.zeros_like(y_ref)


def _moe_ffn(xs, tile_expert, tile_valid, wg, wu, wd, *, tm, fc):
    ne, d, f = wg.shape
    n_tiles = xs.shape[0] // (tm * SUBLANES)
    nj = f // fc
    jj = lambda i, j, te, tv: jnp.where(tv[i] != 0, j, nj - 1)
    rows = pl.BlockSpec((tm * SUBLANES, LANES), lambda i, j, te, tv: (i, 0))
    return pl.pallas_call(
        functools.partial(_moe_ffn_kernel, tm=tm),
        grid_spec=pltpu.PrefetchScalarGridSpec(
            num_scalar_prefetch=2,
            grid=(n_tiles, nj),
            in_specs=[
                rows,
                pl.BlockSpec((1, d, fc), lambda i, j, te, tv: (te[i], 0, jj(i, j, te, tv))),
                pl.BlockSpec((1, d, fc), lambda i, j, te, tv: (te[i], 0, jj(i, j, te, tv))),
                pl.BlockSpec((1, fc, d), lambda i, j, te, tv: (te[i], jj(i, j, te, tv), 0)),
            ],
            out_specs=rows,
            scratch_shapes=[pltpu.VMEM((tm, d), BF16), pltpu.VMEM((tm, d), F32)],
        ),
        out_shape=jax.ShapeDtypeStruct(xs.shape, F32),
        compiler_params=_params("arbitrary", "arbitrary"),
        name="moe_ffn",
    )(tile_expert, tile_valid, xs, wg, wu, wd)


def _moe_combine_kernel(pos_ref, x1_ref, ple_ref, route_ref, yr_ref, g2_ref, b2_ref, out_ref, yg_scr, sem,
                        *, tm, alpha):
    d = x1_ref.shape[1]
    nchunk = d // LANES

    def copy(tok, kk):
        return _row_copy(yr_ref, pos_ref[0, 0, TOP_K * tok + kk], yg_scr, kk * tm + tok, sem)

    def issue(tok, carry):
        for kk in range(TOP_K):
            copy(tok, kk).start()
        return carry

    lax.fori_loop(0, tm, issue, 0, unroll=DMA_UNROLL)

    def drain(tok, carry):
        for kk in range(TOP_K):
            copy(tok, kk).wait()
        return carry

    lax.fori_loop(0, tm, drain, 0, unroll=DMA_UNROLL)
    route = route_ref[...]
    gates = [route[:, TOP_K + kk:TOP_K + kk + 1] for kk in range(TOP_K)]
    chunks = []
    for cc in range(nchunk):
        f = gates[0] * yg_scr[pl.ds(cc, tm, stride=nchunk), :]
        for kk in range(1, TOP_K):
            f = f + gates[kk] * yg_scr[pl.ds(kk * tm * nchunk + cc, tm, stride=nchunk), :]
        chunks.append(f)
    ffn = jnp.concatenate(chunks, axis=1)
    out_ref[...] = _layer_norm(alpha * x1_ref[...] + ffn + ple_ref[...].astype(F32), g2_ref[...], b2_ref[...])


def _moe_combine(x1, ple, route, yr, pos, g2, b2, *, tm, alpha):
    t, d = x1.shape
    row = lambda width: pl.BlockSpec((tm, width), lambda i: (i, 0))
    full = lambda arr: pl.BlockSpec(arr.shape, lambda i: (0,) * arr.ndim)
    pos3 = pos.reshape(t // tm, 1, TOP_K * tm)
    return pl.pallas_call(
        functools.partial(_moe_combine_kernel, tm=tm, alpha=alpha),
        grid=(t // tm,),
        in_specs=[pl.BlockSpec((1, 1, TOP_K * tm), lambda i: (i, 0, 0), memory_space=pltpu.SMEM),
                  row(d), row(d), row(LANES), pl.BlockSpec(memory_space=pl.ANY), full(g2), full(b2)],
        out_specs=row(d),
        out_shape=jax.ShapeDtypeStruct((t, d), F32),
        scratch_shapes=[pltpu.VMEM((TOP_K * tm * SUBLANES, LANES), F32), pltpu.SemaphoreType.DMA],
        compiler_params=_params("arbitrary"),
        name="moe_combine",
    )(pos3, x1, ple, route, yr, g2, b2)


def kernel(x, p, rel_bias, w_in, conv_w, conv_b, conv_ln_g, conv_ln_b, w_conv_proj, w_attn_proj, w_out, ln1_g, ln1_b, w_ple_gate, w_ple_proj, ln2_g, ln2_b, ffn_w_gate, ffn_w_up, ffn_w_down, router_w, exp_w_gate, exp_w_up, exp_w_down):
    bsz, s, d = x.shape
    depth = w_in.shape[0]
    c = conv_w.shape[2]
    a = w_attn_proj.shape[1]
    n_experts = router_w.shape[2]
    assert a == N_HEADS * HEAD_DIM and s % (2 * MOBA_BLOCK) == 0 and conv_w.shape[1] == CONV_KERNEL
    assert n_experts >= TOP_K
    alpha = (2 * depth) ** 0.25
    t = bsz * s
    tm = 512
    tm_combine = 256
    fc = 1792
    assert s % tm == 0 and (TOP_K * t) % tm == 0

    tables = _bias_tables(rel_bias)
    xf = x.reshape(t, d)
    row2 = lambda v: v.reshape(1, -1).astype(F32)
    for i in range(depth):
        h, q, k, v, sgc, sga = _in_proj(xf, w_in[i].astype(BF16), c=c, a=a, tm=tm)
        o = _attention(q, k, v, tables, bsz=bsz, s=s)
        moe = i % 2 == 1
        cw = jnp.pad(conv_w[i], ((0, HALO_ROWS - CONV_KERNEL), (0, 0)))
        rw = jnp.pad(router_w[i // 2], ((0, 0), (0, LANES - n_experts))).astype(BF16) if moe else None
        x1, ple, x1r, route = _mix(xf, h, o, sgc, sga, p[i].reshape(t, -1), cw, row2(conv_b[i]),
                                   row2(conv_ln_g[i]), row2(conv_ln_b[i]),
                                   w_conv_proj[i].astype(BF16), w_attn_proj[i].astype(BF16), w_out[i].astype(BF16),
                                   row2(ln1_g[i]), row2(ln1_b[i]),
                                   w_ple_gate[i].astype(BF16), w_ple_proj[i].astype(BF16), rw,
                                   s=s, tm=tm, alpha=alpha, n_experts=n_experts if moe else 0)
        ln2 = (row2(ln2_g[i]), row2(ln2_b[i]))
        if moe:
            tile_expert, tile_valid, pos, n_tiles = _moe_plan(route, n_experts, tm)
            xs = _moe_dispatch(x1r, pos, n_tiles * tm, tm=tm)
            yr = _moe_ffn(xs, tile_expert, tile_valid, exp_w_gate[i // 2].astype(BF16),
                          exp_w_up[i // 2].astype(BF16), exp_w_down[i // 2].astype(BF16), tm=tm, fc=fc)
            xf = _moe_combine(x1, ple, route, yr, pos, *ln2, tm=tm_combine, alpha=alpha)
        else:
            xf = _ffn(x1, ple, ffn_w_gate[i // 2].astype(BF16), ffn_w_up[i // 2].astype(BF16),
                      ffn_w_down[i // 2].astype(BF16), *ln2, tm=tm, fc=fc, alpha=alpha)
    return xf.reshape(bsz, s, d)
```

```python
import functools
import math

import numpy as np
import jax
import jax.numpy as jnp
from jax import lax
from jax.experimental import pallas as pl
from jax.experimental.pallas import tpu as pltpu

N_HEADS = 8
HEAD_DIM = 64
CONV_KERNEL = 31
MOBA_BLOCK = 256
MOBA_TOPK = 3
REL_BUCKETS = 32
REL_MAX_DIST = 128
TOP_K = 2
LN_EPS = 1e-5

LANES = 128
SUBLANES = 8
BF16_SUBLANES = 16
HALO_ROWS = 32
VMEM_LIMIT = 56 * 1024 * 1024

F32 = jnp.float32
BF16 = jnp.bfloat16
NEG_INF = float("-inf")


def _sigmoid(t):
    return 1.0 / (1.0 + jnp.exp(-t))


def _layer_norm(t, g, b):
    mu = jnp.mean(t, axis=-1, keepdims=True)
    d = t - mu
    var = jnp.mean(d * d, axis=-1, keepdims=True)
    return d * lax.rsqrt(var + LN_EPS) * g + b


def _dot(a, b):
    return jnp.dot(a, b, preferred_element_type=F32)


def _params(*semantics):
    return pltpu.CompilerParams(dimension_semantics=semantics, vmem_limit_bytes=VMEM_LIMIT)


def _rel_bucket(dist):
    n = jnp.maximum(dist, 0)
    max_exact = REL_BUCKETS // 2
    nf = jnp.maximum(n, 1).astype(F32)
    large = max_exact + (jnp.log(nf / max_exact) / math.log(REL_MAX_DIST / max_exact)
                         * (REL_BUCKETS - max_exact)).astype(jnp.int32)
    large = jnp.minimum(large, REL_BUCKETS - 1)
    return jnp.where(n < max_exact, n, large)


def _in_proj_kernel(x_ref, w_ref, h_ref, q_ref, k_ref, v_ref, sgc_ref, sga_ref, *, c, a, d):
    xb = x_ref[...].astype(BF16)
    o = 0
    glu_in = _dot(xb, w_ref[:, o:o + c])
    glu_gate = _dot(xb, w_ref[:, o + c:o + 2 * c])
    h_ref[...] = (glu_in * _sigmoid(glu_gate)).astype(BF16)
    o += 2 * c
    q_ref[...] = _dot(xb, w_ref[:, o:o + a]).astype(BF16)
    o += a
    k_ref[...] = _dot(xb, w_ref[:, o:o + a]).astype(BF16)
    o += a
    v_ref[...] = _dot(xb, w_ref[:, o:o + a]).astype(BF16)
    o += a
    sgc_ref[...] = _sigmoid(_dot(xb, w_ref[:, o:o + d])).astype(BF16)
    o += d
    sga_ref[...] = _sigmoid(_dot(xb, w_ref[:, o:o + d])).astype(BF16)


def _in_proj(x, w, *, c, a, tm):
    t, d = x.shape
    n = w.shape[1]
    row = lambda width: pl.BlockSpec((tm, width), lambda i: (i, 0))
    return pl.pallas_call(
        functools.partial(_in_proj_kernel, c=c, a=a, d=d),
        grid=(t // tm,),
        in_specs=[row(d), pl.BlockSpec((d, n), lambda i: (0, 0))],
        out_specs=[row(c), row(a), row(a), row(a), row(d), row(d)],
        out_shape=[jax.ShapeDtypeStruct((t, c), BF16)] + [jax.ShapeDtypeStruct((t, a), BF16)] * 3
        + [jax.ShapeDtypeStruct((t, d), BF16)] * 2,
        compiler_params=_params("arbitrary"),
        name="in_proj",
    )(x, w)


LOG2E = math.log2(math.e)
MASKED = float(jnp.finfo(jnp.bfloat16).min)
VT_ROWS = HEAD_DIM + BF16_SUBLANES


def _attn_kernel(q_ref, k_ref, v_ref, bias_ref, o_ref, vt_scr, kaug_scr, kmean_scr, sel_scr, sa_scr, sb_scr,
                 *, nb, nbp):
    blk = MOBA_BLOCK
    i = pl.program_id(2)

    @pl.when(i == 0)
    def _():
        if nbp > nb:
            kmean_scr[...] = jnp.zeros_like(kmean_scr)
        col = lax.broadcasted_iota(jnp.int32, (blk, LANES), 1)
        ones = jnp.ones((BF16_SUBLANES, blk), BF16)
        for j in range(nb):
            kb = k_ref[j * blk:(j + 1) * blk, :]
            kmean_scr[j:j + 1, :] = jnp.sum(kb.astype(F32), axis=0, keepdims=True) * (1.0 / blk)
            kaug_scr[j * blk:(j + 1) * blk, 0:LANES] = kb
            kaug_scr[j * blk:(j + 1) * blk, LANES:2 * LANES] = jnp.where(col == j, 1.0, 0.0).astype(BF16)
            vt = v_ref[j * blk:(j + 1) * blk, :].astype(F32).T.astype(BF16)
            for hh in range(2):
                vt_scr[j, hh, 0:HEAD_DIM, :] = vt[hh * HEAD_DIM:(hh + 1) * HEAD_DIM, :]
                vt_scr[j, hh, HEAD_DIM:VT_ROWS, :] = ones

    qt = q_ref[...].astype(F32).T * (HEAD_DIM ** -0.5 * LOG2E)
    qrow = lax.broadcasted_iota(jnp.int32, qt.shape, 0)
    kmean = kmean_scr[...].astype(BF16)
    brow = lax.broadcasted_iota(jnp.int32, (nbp, blk), 0)
    n_far = i - 1
    pad_rows = jnp.zeros((LANES - nbp, blk), BF16)

    qhs, qaugs = [], []
    for hh in range(2):
        lo = hh * HEAD_DIM
        qh = jnp.where((qrow >= lo) & (qrow < lo + HEAD_DIM), qt, 0.0).astype(BF16)
        qhs.append(qh)
        score = _dot(kmean, qh)
        cur = jnp.where(brow < i, score, NEG_INF)
        sel = jnp.zeros((nbp, blk), dtype=jnp.bool_)
        for t in range(MOBA_TOPK):
            mx = jnp.max(cur, axis=0, keepdims=True)
            first = jnp.min(jnp.where(cur == mx, brow, nbp), axis=0, keepdims=True)
            pick = brow == first
            sel = sel | (pick & (t < i))
            cur = jnp.where(pick, NEG_INF, cur)
        sel_scr[hh] = jnp.where(sel, 0.0, NEG_INF)
        far_rows = jnp.where(sel & (brow < n_far), 0.0, MASKED).astype(BF16)
        qaugs.append(jnp.concatenate([qh, far_rows, pad_rows], axis=0))

    def k_block(j):
        return k_ref[pl.ds(pl.multiple_of(j * blk, blk), blk), :]

    def sel_row(j, hh):
        return sel_scr[hh, pl.ds(j, 1), :]

    jp = jnp.maximum(i - 1, 0)
    n_units = (n_far + 1) // 2
    u_last = nb // 2 - 1

    def stage_ab_near(s_buf):
        k_prev, k_own = k_block(jp), k_block(i)
        cms = []
        for hh in range(2):
            s0 = _dot(k_prev, qhs[hh]) + bias_ref[0, hh, 1] + sel_row(jp, hh)
            s1 = _dot(k_own, qhs[hh]) + bias_ref[0, hh, 0]
            s_buf[hh, 0:blk, :] = s0
            s_buf[hh, blk:2 * blk, :] = s1
            cms.append(jnp.maximum(jnp.max(s0, axis=0, keepdims=True), jnp.max(s1, axis=0, keepdims=True)))
        return cms

    def stage_ab(u, s_buf):
        u = jnp.minimum(u, u_last)
        kaug = kaug_scr[pl.ds(pl.multiple_of(u * (2 * blk), 2 * blk), 2 * blk), :]
        cms = []
        for hh in range(2):
            s0 = _dot(kaug[0:blk], qaugs[hh])
            s1 = _dot(kaug[blk:2 * blk], qaugs[hh])
            s_buf[hh, 0:blk, :] = s0
            s_buf[hh, blk:2 * blk, :] = s1
            cms.append(jnp.maximum(jnp.max(s0, axis=0, keepdims=True), jnp.max(s1, axis=0, keepdims=True)))
        return cms

    def stage_c(j0, j1, s_buf, cms, carry):
        out = []
        for hh in range(2):
            m_old, acc_old = carry[2 * hh:2 * hh + 2]
            m_new = jnp.maximum(m_old, cms[hh])
            p0 = jnp.exp2(s_buf[hh, 0:blk, :] - m_new).astype(BF16)
            p1 = jnp.exp2(s_buf[hh, blk:2 * blk, :] - m_new).astype(BF16)
            acc_new = (jnp.exp2(m_old - m_new) * acc_old + _dot(vt_scr[j0, hh], p0) + _dot(vt_scr[j1, hh], p1))
            out += [m_new, acc_new]
        return out

    def stage_c_far(u, s_buf, cms, carry):
        u = jnp.minimum(u, u_last)
        return stage_c(2 * u, 2 * u + 1, s_buf, cms, carry)

    cm_near = stage_ab_near(sa_scr)
    cm_b = stage_ab(0, sb_scr)
    carry = []
    for hh in range(2):
        carry += [jnp.full((1, blk), NEG_INF, F32), jnp.zeros((VT_ROWS, blk), F32)]
    carry = stage_c(jp, i, sa_scr, cm_near, carry)

    def far_body(it, state):
        cm_b, carry = list(state[0:2]), list(state[2:])
        u0 = 2 * it
        cm_a = stage_ab(u0 + 1, sa_scr)
        carry = stage_c_far(u0, sb_scr, cm_b, carry)
        cm_b = stage_ab(u0 + 2, sb_scr)
        carry = stage_c_far(u0 + 1, sa_scr, cm_a, carry)
        return tuple(cm_b + carry)

    state = lax.fori_loop(0, (n_units + 1) // 2, far_body, tuple(cm_b + carry))
    carry = state[2:]
    outs = [carry[2 * hh + 1][0:HEAD_DIM] / carry[2 * hh + 1][HEAD_DIM:HEAD_DIM + 1] for hh in range(2)]
    o_ref[...] = jnp.concatenate(outs, axis=0).T.astype(BF16)


def _attention(q, k, v, bias_tables, *, bsz, s):
    t, a = q.shape
    blk = MOBA_BLOCK
    nb = s // blk
    nbp = -(-nb // BF16_SUBLANES) * BF16_SUBLANES
    assert nbp <= LANES and nb % 2 == 0
    npairs = a // LANES
    return pl.pallas_call(
        functools.partial(_attn_kernel, nb=nb, nbp=nbp),
        grid=(bsz, npairs, nb),
        in_specs=[
            pl.BlockSpec((blk, LANES), lambda b, p, i: (b * nb + i, p)),
            pl.BlockSpec((s, LANES), lambda b, p, i: (b, p)),
            pl.BlockSpec((s, LANES), lambda b, p, i: (b, p)),
            pl.BlockSpec((1, 2, 2, blk, blk), lambda b, p, i: (p, 0, 0, 0, 0)),
        ],
        out_specs=pl.BlockSpec((blk, LANES), lambda b, p, i: (b * nb + i, p)),
        out_shape=jax.ShapeDtypeStruct((t, a), BF16),
        scratch_shapes=[
            pltpu.VMEM((nb, 2, VT_ROWS, blk), BF16),
            pltpu.VMEM((nb * blk, 2 * LANES), BF16),
            pltpu.VMEM((nbp, LANES), F32),
            pltpu.VMEM((2, nbp, blk), F32),
            pltpu.VMEM((2, 2 * blk, blk), F32),
            pltpu.VMEM((2, 2 * blk, blk), F32),
        ],
        compiler_params=_params("arbitrary", "arbitrary", "arbitrary"),
        name="moba_attention",
    )(q, k, v, bias_tables)


def _bias_tables(rel_bias):
    blk = MOBA_BLOCK
    n = np.float32(blk + 1)
    half = REL_BUCKETS // 2
    far_bucket = half + int(np.log(n / np.float32(half)) / math.log(REL_MAX_DIST / half) * (REL_BUCKETS - half))
    assert far_bucket >= REL_BUCKETS - 1
    bias_t = rel_bias.T.astype(F32)
    bias_t = (bias_t - bias_t[:, REL_BUCKETS - 1:]) * LOG2E
    kk = jnp.arange(blk, dtype=jnp.int32)[:, None]
    qq = jnp.arange(blk, dtype=jnp.int32)[None, :]

    def lookup(bucket):
        out = jnp.zeros((bias_t.shape[0],) + bucket.shape, F32)
        for b in range(REL_BUCKETS):
            out = jnp.where(bucket[None] == b, bias_t[:, b][:, None, None], out)
        return out

    d_own = qq - kk
    own = jnp.where(d_own >= 0, lookup(_rel_bucket(d_own)), NEG_INF)
    prev = lookup(_rel_bucket(qq + blk - kk))
    tables = jnp.stack([own, prev], axis=1)
    return tables.reshape(N_HEADS // 2, 2, 2, blk, blk)


def _mix_kernel(x_ref, h_ref, halo_ref, o_ref, sgc_ref, sga_ref, cw_ref, cb_ref, cg_ref, cbeta_ref,
                wcp_ref, wap_ref, wout_ref, g1_ref, b1_ref, *rest, tm, tiles_per_seq, alpha, n_experts):
    if n_experts:
        rw_ref, x1_ref, x1r_ref, route_ref, hc_scr, sh_scr = rest
    else:
        x1_ref, hc_scr, sh_scr = rest
    i = pl.program_id(0)
    first = (i % tiles_per_seq) == 0
    hc_scr[0:HALO_ROWS, :] = jnp.where(first, 0.0, halo_ref[...].astype(F32))
    hc_scr[HALO_ROWS:, :] = h_ref[...].astype(F32)
    span = sh_scr.shape[1]
    for ph in range(1, SUBLANES):
        sh_scr[ph - 1] = hc_scr[ph:ph + span, :]

    def tap(kk):
        off = HALO_ROWS - (CONV_KERNEL - 1) + kk
        ph, start = off % SUBLANES, off - off % SUBLANES
        rows = hc_scr[start:start + tm, :] if ph == 0 else sh_scr[ph - 1, start:start + tm, :]
        return cw_ref[kk:kk + 1, :] * rows

    acc = cb_ref[...] + tap(0)
    for kk in range(1, CONV_KERNEL):
        acc = acc + tap(kk)
    hn = _layer_norm(acc, cg_ref[...], cbeta_ref[...])
    hs = (hn * _sigmoid(hn)).astype(BF16)
    y_conv = _dot(hs, wcp_ref[...])
    y_attn = _dot(o_ref[...], wap_ref[...])
    mixed = sgc_ref[...].astype(F32) * y_conv + sga_ref[...].astype(F32) * y_attn
    z = alpha * x_ref[...] + _dot(mixed.astype(BF16), wout_ref[...])
    x1 = _layer_norm(z, g1_ref[...], b1_ref[...])
    x1_ref[...] = x1
    if n_experts:
        nchunk = x1.shape[1] // LANES
        for cc in range(nchunk):
            x1r_ref[pl.ds(cc, tm, stride=nchunk), :] = x1[:, cc * LANES:(cc + 1) * LANES]
        logits = _dot(x1.astype(BF16), rw_ref[...])
        lane = lax.broadcasted_iota(jnp.int32, logits.shape, 1)
        lg = jnp.where(lane < n_experts, logits, NEG_INF)
        v1 = jnp.max(lg, axis=-1, keepdims=True)
        i1 = jnp.min(jnp.where(lg == v1, lane, LANES), axis=-1, keepdims=True)
        lg2 = jnp.where(lane == i1, NEG_INF, lg)
        v2 = jnp.max(lg2, axis=-1, keepdims=True)
        i2 = jnp.min(jnp.where(lg2 == v2, lane, LANES), axis=-1, keepdims=True)
        e = jnp.exp(v2 - v1)
        route_ref[...] = (jnp.where(lane == 0, i1.astype(F32), 0.0) + jnp.where(lane == 1, i2.astype(F32), 0.0)
                          + jnp.where(lane == 2, 1.0 / (1.0 + e), 0.0) + jnp.where(lane == 3, e / (1.0 + e), 0.0))


def _mix(x, h, o, sgc, sga, cw, cb, cg, cbeta, wcp, wap, wout, g1, b1, rw, *, s, tm, alpha, n_experts):
    t, d = x.shape
    c = h.shape[1]
    a = o.shape[1]
    row = lambda width: pl.BlockSpec((tm, width), lambda i: (i, 0))
    full = lambda arr: pl.BlockSpec(arr.shape, lambda i: (0,) * arr.ndim)
    halo_blocks = tm // HALO_ROWS
    in_specs = [row(d), row(c),
                pl.BlockSpec((HALO_ROWS, c), lambda i: (jnp.maximum(i * halo_blocks - 1, 0), 0)),
                row(a), row(d), row(d),
                full(cw), full(cb), full(cg), full(cbeta), full(wcp), full(wap), full(wout), full(g1), full(b1)]
    args = [x, h, h, o, sgc, sga, cw, cb, cg, cbeta, wcp, wap, wout, g1, b1]
    out_specs = [row(d)]
    out_shape = [jax.ShapeDtypeStruct((t, d), F32)]
    if n_experts:
        assert d == SUBLANES * LANES
        in_specs.append(full(rw))
        args.append(rw)
        out_specs += [pl.BlockSpec((tm * SUBLANES, LANES), lambda i: (i, 0)), row(LANES)]
        out_shape += [jax.ShapeDtypeStruct((t * SUBLANES, LANES), F32), jax.ShapeDtypeStruct((t, LANES), F32)]
    res = pl.pallas_call(
        functools.partial(_mix_kernel, tm=tm, tiles_per_seq=s // tm, alpha=alpha, n_experts=n_experts),
        grid=(t // tm,),
        in_specs=in_specs,
        out_specs=out_specs,
        out_shape=out_shape,
        scratch_shapes=[pltpu.VMEM((tm + HALO_ROWS, c), F32),
                        pltpu.VMEM((SUBLANES - 1, tm + HALO_ROWS - SUBLANES, c), F32)],
        compiler_params=_params("arbitrary"),
        name="mix",
    )(*args)
    return res if n_experts else (res[0], None, None)


def _ple(xb, p_ref, wpg_ref, wpp_ref):
    return _sigmoid(_dot(xb, wpg_ref[...])) * _dot(p_ref[...].astype(BF16), wpp_ref[...])


def _ffn_kernel(x1_ref, p_ref, wg_ref, wu_ref, wd_ref, wpg_ref, wpp_ref, g2_ref, b2_ref, out_ref,
                xb_scr, acc_scr, *, alpha):
    j = pl.program_id(1)

    @pl.when(j == 0)
    def _():
        xb_scr[...] = x1_ref[...].astype(BF16)
        acc_scr[...] = jnp.zeros_like(acc_scr)

    xb = xb_scr[...]
    g = _dot(xb, wg_ref[...])
    u = _dot(xb, wu_ref[...])
    acc_scr[...] += _dot((g * _sigmoid(g) * u).astype(BF16), wd_ref[...])

    @pl.when(j == pl.num_programs(1) - 1)
    def _():
        ple = _ple(xb, p_ref, wpg_ref, wpp_ref)
        out_ref[...] = _layer_norm(alpha * x1_ref[...] + acc_scr[...] + ple, g2_ref[...], b2_ref[...])


def _ffn(x1, p, wg, wu, wd, wpg, wpp, g2, b2, *, tm, fc, alpha):
    t, d = x1.shape
    f = wg.shape[1]
    row = lambda width: pl.BlockSpec((tm, width), lambda i, j: (i, 0))
    full = lambda arr: pl.BlockSpec(arr.shape, lambda i, j: (0,) * arr.ndim)
    return pl.pallas_call(
        functools.partial(_ffn_kernel, alpha=alpha),
        grid=(t // tm, f // fc),
        in_specs=[row(d), row(p.shape[1]),
                  pl.BlockSpec((d, fc), lambda i, j: (0, j)),
                  pl.BlockSpec((d, fc), lambda i, j: (0, j)),
                  pl.BlockSpec((fc, d), lambda i, j: (j, 0)),
                  full(wpg), full(wpp), full(g2), full(b2)],
        out_specs=row(d),
        out_shape=jax.ShapeDtypeStruct((t, d), F32),
        scratch_shapes=[pltpu.VMEM((tm, d), BF16), pltpu.VMEM((tm, d), F32)],
        compiler_params=_params("arbitrary", "arbitrary"),
        name="ffn",
    )(x1, p, wg, wu, wd, wpg, wpp, g2, b2)


DMA_UNROLL = 8
N_DMA_PRIORITIES = 2


def _moe_plan(route, n_experts, tm):
    e_slot = route[:, 0:TOP_K].astype(jnp.int32).reshape(-1)
    n_slots = e_slot.shape[0]
    ids = jnp.arange(n_experts, dtype=jnp.int32)
    onehot = (e_slot[:, None] == ids[None, :]).astype(jnp.int32)
    csum = jnp.cumsum(onehot, axis=0)
    cnt = csum[-1]
    padded = ((cnt + tm - 1) // tm) * tm
    off_end = jnp.cumsum(padded)
    off = off_end - padded
    pos = jnp.sum((csum - onehot + off[None, :]) * onehot, axis=1)
    n_tiles = n_slots // tm + n_experts
    tile_start = jnp.arange(n_tiles, dtype=jnp.int32) * tm
    tile_expert = jnp.minimum(jnp.sum((tile_start[:, None] >= off_end[None, :]).astype(jnp.int32), axis=1),
                              n_experts - 1)
    tile_valid = (tile_start < off_end[-1]).astype(jnp.int32)
    return tile_expert, tile_valid, pos, n_tiles


def _row_copy(src, src_row, dst, dst_row, sem):
    return pltpu.make_async_copy(
        src.at[pl.ds(pl.multiple_of(src_row * SUBLANES, SUBLANES), SUBLANES), :],
        dst.at[pl.ds(pl.multiple_of(dst_row * SUBLANES, SUBLANES), SUBLANES), :], sem)


def _moe_dispatch_kernel(pos_ref, x1r_ref, init_ref, xs_ref, sem, *, tm):
    del init_ref

    def copy(tok, kk):
        return _row_copy(x1r_ref, tok, xs_ref, pos_ref[0, 0, TOP_K * tok + kk], sem)

    def issue(tok, carry):
        for kk in range(TOP_K):
            copy(tok, kk).start(priority=kk % N_DMA_PRIORITIES)
        return carry

    lax.fori_loop(0, tm, issue, 0, unroll=DMA_UNROLL)

    def drain(tok, carry):
        for kk in range(TOP_K):
            copy(tok, kk).wait()
        return carry

    lax.fori_loop(0, tm, drain, 0, unroll=DMA_UNROLL)


def _moe_dispatch(x1r, pos, n_rows, *, tm):
    t = x1r.shape[0] // SUBLANES
    pos3 = pos.reshape(t // tm, 1, TOP_K * tm)
    init = jnp.zeros((n_rows * SUBLANES, LANES), F32)
    return pl.pallas_call(
        functools.partial(_moe_dispatch_kernel, tm=tm),
        grid=(t // tm,),
        in_specs=[pl.BlockSpec((1, 1, TOP_K * tm), lambda i: (i, 0, 0), memory_space=pltpu.SMEM),
                  pl.BlockSpec((tm * SUBLANES, LANES), lambda i: (i, 0)), pl.BlockSpec(memory_space=pl.ANY)],
        out_specs=pl.BlockSpec(memory_space=pl.ANY),
        out_shape=jax.ShapeDtypeStruct(init.shape, F32),
        scratch_shapes=[pltpu.SemaphoreType.DMA],
        input_output_aliases={2: 0},
        compiler_params=_params("arbitrary"),
        name="moe_dispatch",
    )(pos3, x1r, init)


def _moe_ffn_kernel(te_ref, valid_ref, xs_ref, wg_ref, wu_ref, wd_ref, y_ref, xb_scr, acc_scr, *, tm):
    i = pl.program_id(0)
    j = pl.program_id(1)
    valid = valid_ref[i] != 0
    nchunk = xb_scr.shape[1] // LANES

    @pl.when(valid & (j == 0))
    def _():
        for cc in range(nchunk):
            xb_scr[:, cc * LANES:(cc + 1) * LANES] = xs_ref[pl.ds(cc, tm, stride=nchunk), :].astype(BF16)
        acc_scr[...] = jnp.zeros_like(acc_scr)

    @pl.when(valid)
    def _():
        xb = xb_scr[...]
        g = _dot(xb, wg_ref[0])
        u = _dot(xb, wu_ref[0])
        acc_scr[...] += _dot((g * _sigmoid(g) * u).astype(BF16), wd_ref[0])

    last = j == pl.num_programs(1) - 1

    @pl.when(last & valid)
    def _():
        for cc in range(nchunk):
            y_ref[pl.ds(cc, tm, stride=nchunk), :] = acc_scr[:, cc * LANES:(cc + 1) * LANES]

    @pl.when(last & jnp.logical_not(valid))
    def _():
        y_ref[...] = jnp.zeros_like(y_ref)


def _moe_ffn(xs, tile_expert, tile_valid, wg, wu, wd, *, tm, fc):
    ne, d, f = wg.shape
    n_tiles = xs.shape[0] // (tm * SUBLANES)
    nj = f // fc
    jj = lambda i, j, te, tv: jnp.where(tv[i] != 0, j, nj - 1)
    rows = pl.BlockSpec((tm * SUBLANES, LANES), lambda i, j, te, tv: (i, 0))
    return pl.pallas_call(
        functools.partial(_moe_ffn_kernel, tm=tm),
        grid_spec=pltpu.PrefetchScalarGridSpec(
            num_scalar_prefetch=2,
            grid=(n_tiles, nj),
            in_specs=[
                rows,
                pl.BlockSpec((1, d, fc), lambda i, j, te, tv: (te[i], 0, jj(i, j, te, tv))),
                pl.BlockSpec((1, d, fc), lambda i, j, te, tv: (te[i], 0, jj(i, j, te, tv))),
                pl.BlockSpec((1, fc, d), lambda i, j, te, tv: (te[i], jj(i, j, te, tv), 0)),
            ],
            out_specs=rows,
            scratch_shapes=[pltpu.VMEM((tm, d), BF16), pltpu.VMEM((tm, d), F32)],
        ),
        out_shape=jax.ShapeDtypeStruct(xs.shape, F32),
        compiler_params=_params("arbitrary", "arbitrary"),
        name="moe_ffn",
    )(tile_expert, tile_valid, xs, wg, wu, wd)


def _moe_combine_kernel(pos_ref, x1_ref, p_ref, route_ref, yr_ref, wpg_ref, wpp_ref, g2_ref, b2_ref, out_ref,
                        yg_scr, sem, *, tm, alpha):
    d = x1_ref.shape[1]
    nchunk = d // LANES

    def copy(tok, kk):
        return _row_copy(yr_ref, pos_ref[0, 0, TOP_K * tok + kk], yg_scr, kk * tm + tok, sem)

    def issue(tok, carry):
        for kk in range(TOP_K):
            copy(tok, kk).start(priority=kk % N_DMA_PRIORITIES)
        return carry

    lax.fori_loop(0, tm, issue, 0, unroll=DMA_UNROLL)
    x1 = x1_ref[...]
    ple = _ple(x1.astype(BF16), p_ref, wpg_ref, wpp_ref)

    def drain(tok, carry):
        for kk in range(TOP_K):
            copy(tok, kk).wait()
        return carry

    lax.fori_loop(0, tm, drain, 0, unroll=DMA_UNROLL)
    route = route_ref[...]
    gates = [route[:, TOP_K + kk:TOP_K + kk + 1] for kk in range(TOP_K)]
    chunks = []
    for cc in range(nchunk):
        f = gates[0] * yg_scr[pl.ds(cc, tm, stride=nchunk), :]
        for kk in range(1, TOP_K):
            f = f + gates[kk] * yg_scr[pl.ds(kk * tm * nchunk + cc, tm, stride=nchunk), :]
        chunks.append(f)
    ffn = jnp.concatenate(chunks, axis=1)
    out_ref[...] = _layer_norm(alpha * x1 + ffn + ple, g2_ref[...], b2_ref[...])


def _moe_combine(x1, p, route, yr, pos, wpg, wpp, g2, b2, *, tm, alpha):
    t, d = x1.shape
    row = lambda width: pl.BlockSpec((tm, width), lambda i: (i, 0))
    full = lambda arr: pl.BlockSpec(arr.shape, lambda i: (0,) * arr.ndim)
    pos3 = pos.reshape(t // tm, 1, TOP_K * tm)
    return pl.pallas_call(
        functools.partial(_moe_combine_kernel, tm=tm, alpha=alpha),
        grid=(t // tm,),
        in_specs=[pl.BlockSpec((1, 1, TOP_K * tm), lambda i: (i, 0, 0), memory_space=pltpu.SMEM),
                  row(d), row(p.shape[1]), row(LANES), pl.BlockSpec(memory_space=pl.ANY),
                  full(wpg), full(wpp), full(g2), full(b2)],
        out_specs=row(d),
        out_shape=jax.ShapeDtypeStruct((t, d), F32),
        scratch_shapes=[pltpu.VMEM((TOP_K * tm * SUBLANES, LANES), F32), pltpu.SemaphoreType.DMA],
        compiler_params=_params("arbitrary"),
        name="moe_combine",
    )(pos3, x1, p, route, yr, wpg, wpp, g2, b2)


def kernel(x, p, rel_bias, w_in, conv_w, conv_b, conv_ln_g, conv_ln_b, w_conv_proj, w_attn_proj, w_out, ln1_g, ln1_b, w_ple_gate, w_ple_proj, ln2_g, ln2_b, ffn_w_gate, ffn_w_up, ffn_w_down, router_w, exp_w_gate, exp_w_up, exp_w_down):
    bsz, s, d = x.shape
    depth = w_in.shape[0]
    c = conv_w.shape[2]
    a = w_attn_proj.shape[1]
    n_experts = router_w.shape[2]
    assert a == N_HEADS * HEAD_DIM and s % (2 * MOBA_BLOCK) == 0 and conv_w.shape[1] == CONV_KERNEL
    assert n_experts >= TOP_K
    alpha = (2 * depth) ** 0.25
    t = bsz * s
    tm = 512
    tm_combine = 256
    fc = 1792
    assert s % tm == 0 and (TOP_K * t) % tm == 0

    tables = _bias_tables(rel_bias)
    xf = x.reshape(t, d)
    row2 = lambda v: v.reshape(1, -1).astype(F32)
    for i in range(depth):
        h, q, k, v, sgc, sga = _in_proj(xf, w_in[i].astype(BF16), c=c, a=a, tm=tm)
        o = _attention(q, k, v, tables, bsz=bsz, s=s)
        moe = i % 2 == 1
        cw = jnp.pad(conv_w[i], ((0, HALO_ROWS - CONV_KERNEL), (0, 0)))
        rw = jnp.pad(router_w[i // 2], ((0, 0), (0, LANES - n_experts))).astype(BF16) if moe else None
        x1, x1r, route = _mix(xf, h, o, sgc, sga, cw, row2(conv_b[i]), row2(conv_ln_g[i]), row2(conv_ln_b[i]),
                              w_conv_proj[i].astype(BF16), w_attn_proj[i].astype(BF16), w_out[i].astype(BF16),
                              row2(ln1_g[i]), row2(ln1_b[i]), rw,
                              s=s, tm=tm, alpha=alpha, n_experts=n_experts if moe else 0)
        tail = (w_ple_gate[i].astype(BF16), w_ple_proj[i].astype(BF16), row2(ln2_g[i]), row2(ln2_b[i]))
        p_i = p[i].reshape(t, -1)
        if moe:
            tile_expert, tile_valid, pos, n_tiles = _moe_plan(route, n_experts, tm)
            xs = _moe_dispatch(x1r, pos, n_tiles * tm, tm=tm)
            yr = _moe_ffn(xs, tile_expert, tile_valid, exp_w_gate[i // 2].astype(BF16),
                          exp_w_up[i // 2].astype(BF16), exp_w_down[i // 2].astype(BF16), tm=tm, fc=fc)
            xf = _moe_combine(x1, p_i, route, yr, pos, *tail, tm=tm_combine, alpha=alpha)
        else:
            xf = _ffn(x1, p_i, ffn_w_gate[i // 2].astype(BF16), ffn_w_up[i // 2].astype(BF16),
                      ffn_w_down[i // 2].astype(BF16), *tail, tm=tm, fc=fc, alpha=alpha)
    return xf.reshape(bsz, s, d)
```

```python
import functools
import math

import numpy as np
import jax
import jax.numpy as jnp
from jax import lax
from jax.experimental import pallas as pl
from jax.experimental.pallas import tpu as pltpu

N_HEADS = 8
HEAD_DIM = 64
CONV_KERNEL = 31
MOBA_BLOCK = 256
MOBA_TOPK = 3
REL_BUCKETS = 32
REL_MAX_DIST = 128
TOP_K = 2
LN_EPS = 1e-5

LANES = 128
SUBLANES = 8
BF16_SUBLANES = 16
HALO_ROWS = 32
VMEM_LIMIT = 56 * 1024 * 1024

F32 = jnp.float32
BF16 = jnp.bfloat16
NEG_INF = float("-inf")


def _sigmoid(t):
    return 1.0 / (1.0 + jnp.exp(-t))


def _layer_norm(t, g, b):
    mu = jnp.mean(t, axis=-1, keepdims=True)
    d = t - mu
    var = jnp.mean(d * d, axis=-1, keepdims=True)
    return d * lax.rsqrt(var + LN_EPS) * g + b


def _dot(a, b):
    return jnp.dot(a, b, preferred_element_type=F32)


def _params(*semantics):
    return pltpu.CompilerParams(dimension_semantics=semantics, vmem_limit_bytes=VMEM_LIMIT)


def _rel_bucket(dist):
    n = jnp.maximum(dist, 0)
    max_exact = REL_BUCKETS // 2
    nf = jnp.maximum(n, 1).astype(F32)
    large = max_exact + (jnp.log(nf / max_exact) / math.log(REL_MAX_DIST / max_exact)
                         * (REL_BUCKETS - max_exact)).astype(jnp.int32)
    large = jnp.minimum(large, REL_BUCKETS - 1)
    return jnp.where(n < max_exact, n, large)


def _in_proj_kernel(x_ref, w_ref, h_ref, q_ref, k_ref, v_ref, sgc_ref, sga_ref, *, c, a, d):
    xb = x_ref[...].astype(BF16)
    o = 0
    glu_in = _dot(xb, w_ref[:, o:o + c])
    glu_gate = _dot(xb, w_ref[:, o + c:o + 2 * c])
    h_ref[...] = (glu_in * _sigmoid(glu_gate)).astype(BF16)
    o += 2 * c
    q_ref[...] = _dot(xb, w_ref[:, o:o + a]).astype(BF16)
    o += a
    k_ref[...] = _dot(xb, w_ref[:, o:o + a]).astype(BF16)
    o += a
    v_ref[...] = _dot(xb, w_ref[:, o:o + a]).astype(BF16)
    o += a
    sgc_ref[...] = _sigmoid(_dot(xb, w_ref[:, o:o + d])).astype(BF16)
    o += d
    sga_ref[...] = _sigmoid(_dot(xb, w_ref[:, o:o + d])).astype(BF16)


def _in_proj(x, w, *, c, a, tm):
    t, d = x.shape
    n = w.shape[1]
    row = lambda width: pl.BlockSpec((tm, width), lambda i: (i, 0))
    return pl.pallas_call(
        functools.partial(_in_proj_kernel, c=c, a=a, d=d),
        grid=(t // tm,),
        in_specs=[row(d), pl.BlockSpec((d, n), lambda i: (0, 0))],
        out_specs=[row(c), row(a), row(a), row(a), row(d), row(d)],
        out_shape=[jax.ShapeDtypeStruct((t, c), BF16)] + [jax.ShapeDtypeStruct((t, a), BF16)] * 3
        + [jax.ShapeDtypeStruct((t, d), BF16)] * 2,
        compiler_params=_params("arbitrary"),
        name="in_proj",
    )(x, w)


LOG2E = math.log2(math.e)
MASKED = float(jnp.finfo(jnp.bfloat16).min)
VT_ROWS = HEAD_DIM + BF16_SUBLANES


def _attn_kernel(q_ref, k_ref, v_ref, bias_ref, o_ref, vt_scr, kaug_scr, kmean_scr, sel_scr, sa_scr, sb_scr,
                 *, nb, nbp):
    blk = MOBA_BLOCK
    a = pl.program_id(2)

    @pl.when(a == 0)
    def _():
        if nbp > nb:
            kmean_scr[...] = jnp.zeros_like(kmean_scr)
        col = lax.broadcasted_iota(jnp.int32, (blk, LANES), 1)
        ones = jnp.ones((BF16_SUBLANES, blk), BF16)
        for j in range(nb):
            kb = k_ref[j * blk:(j + 1) * blk, :]
            kmean_scr[j:j + 1, :] = jnp.sum(kb.astype(F32), axis=0, keepdims=True) * (1.0 / blk)
            kaug_scr[j * blk:(j + 1) * blk, 0:LANES] = kb
            kaug_scr[j * blk:(j + 1) * blk, LANES:2 * LANES] = jnp.where(col == j, 1.0, 0.0).astype(BF16)
            vt = v_ref[j * blk:(j + 1) * blk, :].astype(F32).T.astype(BF16)
            for hh in range(2):
                vt_scr[j, hh, 0:HEAD_DIM, :] = vt[hh * HEAD_DIM:(hh + 1) * HEAD_DIM, :]
                vt_scr[j, hh, HEAD_DIM:VT_ROWS, :] = ones

    kmean = kmean_scr[...].astype(BF16)
    brow = lax.broadcasted_iota(jnp.int32, (nbp, blk), 0)
    pad_rows = jnp.zeros((LANES - nbp, blk), BF16)
    qrow = lax.broadcasted_iota(jnp.int32, (LANES, blk), 0)

    chains = [(tile, hh) for tile in range(2) for hh in range(2)]
    own = [2 * a, 2 * a + 1]
    prev = [jnp.maximum(2 * a - 1, 0), 2 * a]
    n_far = [2 * a - 1, 2 * a]
    qhs, qaugs = [], []
    for tile in range(2):
        qt = q_ref[tile * blk:(tile + 1) * blk, :].astype(F32).T * (HEAD_DIM ** -0.5 * LOG2E)
        for hh in range(2):
            lo = hh * HEAD_DIM
            qh = jnp.where((qrow >= lo) & (qrow < lo + HEAD_DIM), qt, 0.0).astype(BF16)
            qhs.append(qh)
            score = _dot(kmean, qh)
            cur = jnp.where(brow < own[tile], score, NEG_INF)
            sel = jnp.zeros((nbp, blk), dtype=jnp.bool_)
            for t in range(MOBA_TOPK):
                mx = jnp.max(cur, axis=0, keepdims=True)
                first = jnp.min(jnp.where(cur == mx, brow, nbp), axis=0, keepdims=True)
                pick = brow == first
                sel = sel | (pick & (t < own[tile]))
                cur = jnp.where(pick, NEG_INF, cur)
            sel_scr[2 * tile + hh] = jnp.where(sel, 0.0, NEG_INF)
            far_rows = jnp.where(sel & (brow < n_far[tile]), 0.0, MASKED).astype(BF16)
            qaugs.append(jnp.concatenate([qh, far_rows, pad_rows], axis=0))

    def k_block(j):
        return k_ref[pl.ds(pl.multiple_of(j * blk, blk), blk), :]

    def sel_row(j, c):
        return sel_scr[c, pl.ds(j, 1), :]

    n_units = a
    u_last = nb // 2 - 1
    nc = len(chains)

    def colmax(s0, s1):
        return jnp.maximum(jnp.max(s0, axis=0, keepdims=True), jnp.max(s1, axis=0, keepdims=True))

    def stage_ab_near(s_buf):
        cms = []
        for c, (tile, hh) in enumerate(chains):
            s0 = _dot(k_block(prev[tile]), qhs[c]) + bias_ref[0, hh, 1] + sel_row(prev[tile], c)
            s1 = _dot(k_block(own[tile]), qhs[c]) + bias_ref[0, hh, 0]
            s_buf[c, 0:blk, :] = s0
            s_buf[c, blk:2 * blk, :] = s1
            cms.append(colmax(s0, s1))
        return cms

    def stage_ab(u, s_buf):
        u = jnp.minimum(u, u_last)
        kaug = kaug_scr[pl.ds(pl.multiple_of(u * (2 * blk), 2 * blk), 2 * blk), :]
        cms = []
        for c in range(nc):
            s0 = _dot(kaug[0:blk], qaugs[c])
            s1 = _dot(kaug[blk:2 * blk], qaugs[c])
            s_buf[c, 0:blk, :] = s0
            s_buf[c, blk:2 * blk, :] = s1
            cms.append(colmax(s0, s1))
        return cms

    def stage_c(blocks, s_buf, cms, carry):
        out = []
        for c, (tile, hh) in enumerate(chains):
            j0, j1 = blocks(tile)
            m_old, acc_old = carry[2 * c:2 * c + 2]
            m_new = jnp.maximum(m_old, cms[c])
            p0 = jnp.exp2(s_buf[c, 0:blk, :] - m_new).astype(BF16)
            p1 = jnp.exp2(s_buf[c, blk:2 * blk, :] - m_new).astype(BF16)
            acc_new = (jnp.exp2(m_old - m_new) * acc_old + _dot(vt_scr[j0, hh], p0) + _dot(vt_scr[j1, hh], p1))
            out += [m_new, acc_new]
        return out

    def stage_c_far(u, s_buf, cms, carry):
        u = jnp.minimum(u, u_last)
        return stage_c(lambda tile: (2 * u, 2 * u + 1), s_buf, cms, carry)

    cm_near = stage_ab_near(sa_scr)
    cm_b = stage_ab(0, sb_scr)
    carry = []
    for c in range(nc):
        carry += [jnp.full((1, blk), NEG_INF, F32), jnp.zeros((VT_ROWS, blk), F32)]
    carry = stage_c(lambda tile: (prev[tile], own[tile]), sa_scr, cm_near, carry)

    def far_body(it, state):
        cm_b, carry = list(state[0:nc]), list(state[nc:])
        u0 = 2 * it
        cm_a = stage_ab(u0 + 1, sa_scr)
        carry = stage_c_far(u0, sb_scr, cm_b, carry)
        cm_b = stage_ab(u0 + 2, sb_scr)
        carry = stage_c_far(u0 + 1, sa_scr, cm_a, carry)
        return tuple(cm_b + carry)

    state = lax.fori_loop(0, (n_units + 1) // 2, far_body, tuple(cm_b + carry))
    carry = state[nc:]
    for tile in range(2):
        outs = []
        for hh in range(2):
            acc = carry[2 * (2 * tile + hh) + 1]
            outs.append(acc[0:HEAD_DIM] / acc[HEAD_DIM:HEAD_DIM + 1])
        o_ref[tile * blk:(tile + 1) * blk, :] = jnp.concatenate(outs, axis=0).T.astype(BF16)


def _attention(q, k, v, bias_tables, *, bsz, s):
    t, a = q.shape
    blk = MOBA_BLOCK
    nb = s // blk
    nbp = -(-nb // BF16_SUBLANES) * BF16_SUBLANES
    assert nbp <= LANES and nb % 2 == 0
    npairs = a // LANES
    steps = nb // 2
    return pl.pallas_call(
        functools.partial(_attn_kernel, nb=nb, nbp=nbp),
        grid=(bsz, npairs, steps),
        in_specs=[
            pl.BlockSpec((2 * blk, LANES), lambda b, p, i: (b * steps + i, p)),
            pl.BlockSpec((s, LANES), lambda b, p, i: (b, p)),
            pl.BlockSpec((s, LANES), lambda b, p, i: (b, p)),
            pl.BlockSpec((1, 2, 2, blk, blk), lambda b, p, i: (p, 0, 0, 0, 0)),
        ],
        out_specs=pl.BlockSpec((2 * blk, LANES), lambda b, p, i: (b * steps + i, p)),
        out_shape=jax.ShapeDtypeStruct((t, a), BF16),
        scratch_shapes=[
            pltpu.VMEM((nb, 2, VT_ROWS, blk), BF16),
            pltpu.VMEM((nb * blk, 2 * LANES), BF16),
            pltpu.VMEM((nbp, LANES), F32),
            pltpu.VMEM((4, nbp, blk), F32),
            pltpu.VMEM((4, 2 * blk, blk), F32),
            pltpu.VMEM((4, 2 * blk, blk), F32),
        ],
        compiler_params=_params("arbitrary", "arbitrary", "arbitrary"),
        name="moba_attention",
    )(q, k, v, bias_tables)


def _bias_tables(rel_bias):
    blk = MOBA_BLOCK
    n = np.float32(blk + 1)
    half = REL_BUCKETS // 2
    far_bucket = half + int(np.log(n / np.float32(half)) / math.log(REL_MAX_DIST / half) * (REL_BUCKETS - half))
    assert far_bucket >= REL_BUCKETS - 1
    bias_t = rel_bias.T.astype(F32)
    bias_t = (bias_t - bias_t[:, REL_BUCKETS - 1:]) * LOG2E
    kk = jnp.arange(blk, dtype=jnp.int32)[:, None]
    qq = jnp.arange(blk, dtype=jnp.int32)[None, :]

    def lookup(bucket):
        out = jnp.zeros((bias_t.shape[0],) + bucket.shape, F32)
        for b in range(REL_BUCKETS):
            out = jnp.where(bucket[None] == b, bias_t[:, b][:, None, None], out)
        return out

    d_own = qq - kk
    own = jnp.where(d_own >= 0, lookup(_rel_bucket(d_own)), NEG_INF)
    prev = lookup(_rel_bucket(qq + blk - kk))
    tables = jnp.stack([own, prev], axis=1)
    return tables.reshape(N_HEADS // 2, 2, 2, blk, blk)


def _mix_kernel(x_ref, h_ref, halo_ref, o_ref, sgc_ref, sga_ref, cw_ref, cb_ref, cg_ref, cbeta_ref,
                wcp_ref, wap_ref, wout_ref, g1_ref, b1_ref, *rest, tm, tiles_per_seq, alpha, n_experts):
    if n_experts:
        rw_ref, x1_ref, x1r_ref, route_ref, hc_scr, sh_scr = rest
    else:
        x1_ref, hc_scr, sh_scr = rest
    i = pl.program_id(0)
    first = (i % tiles_per_seq) == 0
    hc_scr[0:HALO_ROWS, :] = jnp.where(first, 0.0, halo_ref[...].astype(F32))
    hc_scr[HALO_ROWS:, :] = h_ref[...].astype(F32)
    span = sh_scr.shape[1]
    for ph in range(1, SUBLANES):
        sh_scr[ph - 1] = hc_scr[ph:ph + span, :]

    def tap(kk):
        off = HALO_ROWS - (CONV_KERNEL - 1) + kk
        ph, start = off % SUBLANES, off - off % SUBLANES
        rows = hc_scr[start:start + tm, :] if ph == 0 else sh_scr[ph - 1, start:start + tm, :]
        return cw_ref[kk:kk + 1, :] * rows

    acc = cb_ref[...] + tap(0)
    for kk in range(1, CONV_KERNEL):
        acc = acc + tap(kk)
    hn = _layer_norm(acc, cg_ref[...], cbeta_ref[...])
    hs = (hn * _sigmoid(hn)).astype(BF16)
    y_conv = _dot(hs, wcp_ref[...])
    y_attn = _dot(o_ref[...], wap_ref[...])
    mixed = sgc_ref[...].astype(F32) * y_conv + sga_ref[...].astype(F32) * y_attn
    z = alpha * x_ref[...] + _dot(mixed.astype(BF16), wout_ref[...])
    x1 = _layer_norm(z, g1_ref[...], b1_ref[...])
    x1_ref[...] = x1
    if n_experts:
        nchunk = x1.shape[1] // LANES
        for cc in range(nchunk):
            x1r_ref[pl.ds(cc, tm, stride=nchunk), :] = x1[:, cc * LANES:(cc + 1) * LANES]
        logits = _dot(x1.astype(BF16), rw_ref[...])
        lane = lax.broadcasted_iota(jnp.int32, logits.shape, 1)
        lg = jnp.where(lane < n_experts, logits, NEG_INF)
        v1 = jnp.max(lg, axis=-1, keepdims=True)
        i1 = jnp.min(jnp.where(lg == v1, lane, LANES), axis=-1, keepdims=True)
        lg2 = jnp.where(lane == i1, NEG_INF, lg)
        v2 = jnp.max(lg2, axis=-1, keepdims=True)
        i2 = jnp.min(jnp.where(lg2 == v2, lane, LANES), axis=-1, keepdims=True)
        e = jnp.exp(v2 - v1)
        route_ref[...] = (jnp.where(lane == 0, i1.astype(F32), 0.0) + jnp.where(lane == 1, i2.astype(F32), 0.0)
                          + jnp.where(lane == 2, 1.0 / (1.0 + e), 0.0) + jnp.where(lane == 3, e / (1.0 + e), 0.0))


def _mix(x, h, o, sgc, sga, cw, cb, cg, cbeta, wcp, wap, wout, g1, b1, rw, *, s, tm, alpha, n_experts):
    t, d = x.shape
    c = h.shape[1]
    a = o.shape[1]
    row = lambda width: pl.BlockSpec((tm, width), lambda i: (i, 0))
    full = lambda arr: pl.BlockSpec(arr.shape, lambda i: (0,) * arr.ndim)
    halo_blocks = tm // HALO_ROWS
    in_specs = [row(d), row(c),
                pl.BlockSpec((HALO_ROWS, c), lambda i: (jnp.maximum(i * halo_blocks - 1, 0), 0)),
                row(a), row(d), row(d),
                full(cw), full(cb), full(cg), full(cbeta), full(wcp), full(wap), full(wout), full(g1), full(b1)]
    args = [x, h, h, o, sgc, sga, cw, cb, cg, cbeta, wcp, wap, wout, g1, b1]
    out_specs = [row(d)]
    out_shape = [jax.ShapeDtypeStruct((t, d), F32)]
    if n_experts:
        assert d == SUBLANES * LANES
        in_specs.append(full(rw))
        args.append(rw)
        out_specs += [pl.BlockSpec((tm * SUBLANES, LANES), lambda i: (i, 0)), row(LANES)]
        out_shape += [jax.ShapeDtypeStruct((t * SUBLANES, LANES), F32), jax.ShapeDtypeStruct((t, LANES), F32)]
    res = pl.pallas_call(
        functools.partial(_mix_kernel, tm=tm, tiles_per_seq=s // tm, alpha=alpha, n_experts=n_experts),
        grid=(t // tm,),
        in_specs=in_specs,
        out_specs=out_specs,
        out_shape=out_shape,
        scratch_shapes=[pltpu.VMEM((tm + HALO_ROWS, c), F32),
                        pltpu.VMEM((SUBLANES - 1, tm + HALO_ROWS - SUBLANES, c), F32)],
        compiler_params=_params("arbitrary"),
        name="mix",
    )(*args)
    return res if n_experts else (res[0], None, None)


def _ple(xb, p_ref, wpg_ref, wpp_ref):
    return _sigmoid(_dot(xb, wpg_ref[...])) * _dot(p_ref[...].astype(BF16), wpp_ref[...])


def _ffn_kernel(x1_ref, p_ref, wg_ref, wu_ref, wd_ref, wpg_ref, wpp_ref, g2_ref, b2_ref, out_ref,
                xb_scr, acc_scr, *, alpha):
    j = pl.program_id(1)

    @pl.when(j == 0)
    def _():
        xb_scr[...] = x1_ref[...].astype(BF16)
        acc_scr[...] = jnp.zeros_like(acc_scr)

    xb = xb_scr[...]
    g = _dot(xb, wg_ref[...])
    u = _dot(xb, wu_ref[...])
    acc_scr[...] += _dot((g * _sigmoid(g) * u).astype(BF16), wd_ref[...])

    @pl.when(j == pl.num_programs(1) - 1)
    def _():
        ple = _ple(xb, p_ref, wpg_ref, wpp_ref)
        out_ref[...] = _layer_norm(alpha * x1_ref[...] + acc_scr[...] + ple, g2_ref[...], b2_ref[...])


def _ffn(x1, p, wg, wu, wd, wpg, wpp, g2, b2, *, tm, fc, alpha):
    t, d = x1.shape
    f = wg.shape[1]
    row = lambda width: pl.BlockSpec((tm, width), lambda i, j: (i, 0))
    full = lambda arr: pl.BlockSpec(arr.shape, lambda i, j: (0,) * arr.ndim)
    return pl.pallas_call(
        functools.partial(_ffn_kernel, alpha=alpha),
        grid=(t // tm, f // fc),
        in_specs=[row(d), row(p.shape[1]),
                  pl.BlockSpec((d, fc), lambda i, j: (0, j)),
                  pl.BlockSpec((d, fc), lambda i, j: (0, j)),
                  pl.BlockSpec((fc, d), lambda i, j: (j, 0)),
                  full(wpg), full(wpp), full(g2), full(b2)],
        out_specs=row(d),
        out_shape=jax.ShapeDtypeStruct((t, d), F32),
        scratch_shapes=[pltpu.VMEM((tm, d), BF16), pltpu.VMEM((tm, d), F32)],
        compiler_params=_params("arbitrary", "arbitrary"),
        name="ffn",
    )(x1, p, wg, wu, wd, wpg, wpp, g2, b2)


DMA_UNROLL = 8
N_DMA_PRIORITIES = 2


def _moe_plan(route, n_experts, tm):
    e_slot = route[:, 0:TOP_K].astype(jnp.int32).reshape(-1)
    n_slots = e_slot.shape[0]
    ids = jnp.arange(n_experts, dtype=jnp.int32)
    onehot = (e_slot[:, None] == ids[None, :]).astype(jnp.int32)
    csum = jnp.cumsum(onehot, axis=0)
    cnt = csum[-1]
    padded = ((cnt + tm - 1) // tm) * tm
    off_end = jnp.cumsum(padded)
    off = off_end - padded
    pos = jnp.sum((csum - onehot + off[None, :]) * onehot, axis=1)
    n_tiles = n_slots // tm + n_experts
    tile_start = jnp.arange(n_tiles, dtype=jnp.int32) * tm
    tile_expert = jnp.minimum(jnp.sum((tile_start[:, None] >= off_end[None, :]).astype(jnp.int32), axis=1),
                              n_experts - 1)
    tile_valid = (tile_start < off_end[-1]).astype(jnp.int32)
    return tile_expert, tile_valid, pos, n_tiles


def _row_copy(src, src_row, dst, dst_row, sem):
    return pltpu.make_async_copy(
        src.at[pl.ds(pl.multiple_of(src_row * SUBLANES, SUBLANES), SUBLANES), :],
        dst.at[pl.ds(pl.multiple_of(dst_row * SUBLANES, SUBLANES), SUBLANES), :], sem)


def _moe_dispatch_kernel(pos_ref, x1r_ref, init_ref, xs_ref, sem, *, tm):
    del init_ref

    def copy(tok, kk):
        return _row_copy(x1r_ref, tok, xs_ref, pos_ref[0, 0, TOP_K * tok + kk], sem)

    def issue(tok, carry):
        for kk in range(TOP_K):
            copy(tok, kk).start(priority=kk % N_DMA_PRIORITIES)
        return carry

    lax.fori_loop(0, tm, issue, 0, unroll=DMA_UNROLL)

    def drain(tok, carry):
        for kk in range(TOP_K):
            copy(tok, kk).wait()
        return carry

    lax.fori_loop(0, tm, drain, 0, unroll=DMA_UNROLL)


def _moe_dispatch(x1r, pos, n_rows, *, tm):
    t = x1r.shape[0] // SUBLANES
    pos3 = pos.reshape(t // tm, 1, TOP_K * tm)
    init = jnp.zeros((n_rows * SUBLANES, LANES), F32)
    return pl.pallas_call(
        functools.partial(_moe_dispatch_kernel, tm=tm),
        grid=(t // tm,),
        in_specs=[pl.BlockSpec((1, 1, TOP_K * tm), lambda i: (i, 0, 0), memory_space=pltpu.SMEM),
                  pl.BlockSpec((tm * SUBLANES, LANES), lambda i: (i, 0)), pl.BlockSpec(memory_space=pl.ANY)],
        out_specs=pl.BlockSpec(memory_space=pl.ANY),
        out_shape=jax.ShapeDtypeStruct(init.shape, F32),
        scratch_shapes=[pltpu.SemaphoreType.DMA],
        input_output_aliases={2: 0},
        compiler_params=_params("arbitrary"),
        name="moe_dispatch",
    )(pos3, x1r, init)


def _moe_ffn_kernel(te_ref, valid_ref, xs_ref, wg_ref, wu_ref, wd_ref, y_ref, xb_scr, acc_scr, *, tm):
    i = pl.program_id(0)
    j = pl.program_id(1)
    valid = valid_ref[i] != 0
    nchunk = xb_scr.shape[1] // LANES

    @pl.when(valid & (j == 0))
    def _():
        for cc in range(nchunk):
            xb_scr[:, cc * LANES:(cc + 1) * LANES] = xs_ref[pl.ds(cc, tm, stride=nchunk), :].astype(BF16)
        acc_scr[...] = jnp.zeros_like(acc_scr)

    @pl.when(valid)
    def _():
        xb = xb_scr[...]
        g = _dot(xb, wg_ref[0])
        u = _dot(xb, wu_ref[0])
        acc_scr[...] += _dot((g * _sigmoid(g) * u).astype(BF16), wd_ref[0])

    last = j == pl.num_programs(1) - 1

    @pl.when(last & valid)
    def _():
        for cc in range(nchunk):
            y_ref[pl.ds(cc, tm, stride=nchunk), :] = acc_scr[:, cc * LANES:(cc + 1) * LANES]

    @pl.when(last & jnp.logical_not(valid))
    def _():
        y_ref[...] = jnp.zeros_like(y_ref)


def _moe_ffn(xs, tile_expert, tile_valid, wg, wu, wd, *, tm, fc):
    ne, d, f = wg.shape
    n_tiles = xs.shape[0] // (tm * SUBLANES)
    nj = f // fc
    jj = lambda i, j, te, tv: jnp.where(tv[i] != 0, j, nj - 1)
    rows = pl.BlockSpec((tm * SUBLANES, LANES), lambda i, j, te, tv: (i, 0))
    return pl.pallas_call(
        functools.partial(_moe_ffn_kernel, tm=tm),
        grid_spec=pltpu.PrefetchScalarGridSpec(
            num_scalar_prefetch=2,
            grid=(n_tiles, nj),
            in_specs=[
                rows,
                pl.BlockSpec((1, d, fc), lambda i, j, te, tv: (te[i], 0, jj(i, j, te, tv))),
                pl.BlockSpec((1, d, fc), lambda i, j, te, tv: (te[i], 0, jj(i, j, te, tv))),
                pl.BlockSpec((1, fc, d), lambda i, j, te, tv: (te[i], jj(i, j, te, tv), 0)),
            ],
            out_specs=rows,
            scratch_shapes=[pltpu.VMEM((tm, d), BF16), pltpu.VMEM((tm, d), F32)],
        ),
        out_shape=jax.ShapeDtypeStruct(xs.shape, F32),
        compiler_params=_params("arbitrary", "arbitrary"),
        name="moe_ffn",
    )(tile_expert, tile_valid, xs, wg, wu, wd)


def _moe_combine_kernel(pos_ref, x1_ref, p_ref, route_ref, yr_ref, wpg_ref, wpp_ref, g2_ref, b2_ref, out_ref,
                        yg_scr, sem, *, tm, alpha):
    d = x1_ref.shape[1]
    nchunk = d // LANES

    def copy(tok, kk):
        return _row_copy(yr_ref, pos_ref[0, 0, TOP_K * tok + kk], yg_scr, kk * tm + tok, sem)

    def issue(tok, carry):
        for kk in range(TOP_K):
            copy(tok, kk).start(priority=kk % N_DMA_PRIORITIES)
        return carry

    lax.fori_loop(0, tm, issue, 0, unroll=DMA_UNROLL)
    x1 = x1_ref[...]
    ple = _ple(x1.astype(BF16), p_ref, wpg_ref, wpp_ref)

    def drain(tok, carry):
        for kk in range(TOP_K):
            copy(tok, kk).wait()
        return carry

    lax.fori_loop(0, tm, drain, 0, unroll=DMA_UNROLL)
    route = route_ref[...]
    gates = [route[:, TOP_K + kk:TOP_K + kk + 1] for kk in range(TOP_K)]
    chunks = []
    for cc in range(nchunk):
        f = gates[0] * yg_scr[pl.ds(cc, tm, stride=nchunk), :]
        for kk in range(1, TOP_K):
            f = f + gates[kk] * yg_scr[pl.ds(kk * tm * nchunk + cc, tm, stride=nchunk), :]
        chunks.append(f)
    ffn = jnp.concatenate(chunks, axis=1)
    out_ref[...] = _layer_norm(alpha * x1 + ffn + ple, g2_ref[...], b2_ref[...])


def _moe_combine(x1, p, route, yr, pos, wpg, wpp, g2, b2, *, tm, alpha):
    t, d = x1.shape
    row = lambda width: pl.BlockSpec((tm, width), lambda i: (i, 0))
    full = lambda arr: pl.BlockSpec(arr.shape, lambda i: (0,) * arr.ndim)
    pos3 = pos.reshape(t // tm, 1, TOP_K * tm)
    return pl.pallas_call(
        functools.partial(_moe_combine_kernel, tm=tm, alpha=alpha),
        grid=(t // tm,),
        in_specs=[pl.BlockSpec((1, 1, TOP_K * tm), lambda i: (i, 0, 0), memory_space=pltpu.SMEM),
                  row(d), row(p.shape[1]), row(LANES), pl.BlockSpec(memory_space=pl.ANY),
                  full(wpg), full(wpp), full(g2), full(b2)],
        out_specs=row(d),
        out_shape=jax.ShapeDtypeStruct((t, d), F32),
        scratch_shapes=[pltpu.VMEM((TOP_K * tm * SUBLANES, LANES), F32), pltpu.SemaphoreType.DMA],
        compiler_params=_params("arbitrary"),
        name="moe_combine",
    )(pos3, x1, p, route, yr, wpg, wpp, g2, b2)


def kernel(x, p, rel_bias, w_in, conv_w, conv_b, conv_ln_g, conv_ln_b, w_conv_proj, w_attn_proj, w_out, ln1_g, ln1_b, w_ple_gate, w_ple_proj, ln2_g, ln2_b, ffn_w_gate, ffn_w_up, ffn_w_down, router_w, exp_w_gate, exp_w_up, exp_w_down):
    bsz, s, d = x.shape
    depth = w_in.shape[0]
    c = conv_w.shape[2]
    a = w_attn_proj.shape[1]
    n_experts = router_w.shape[2]
    assert a == N_HEADS * HEAD_DIM and s % (2 * MOBA_BLOCK) == 0 and conv_w.shape[1] == CONV_KERNEL
    assert n_experts >= TOP_K
    alpha = (2 * depth) ** 0.25
    t = bsz * s
    tm = 512
    tm_combine = 256
    fc = 1792
    assert s % tm == 0 and (TOP_K * t) % tm == 0

    tables = _bias_tables(rel_bias)
    xf = x.reshape(t, d)
    row2 = lambda v: v.reshape(1, -1).astype(F32)
    for i in range(depth):
        h, q, k, v, sgc, sga = _in_proj(xf, w_in[i].astype(BF16), c=c, a=a, tm=tm)
        o = _attention(q, k, v, tables, bsz=bsz, s=s)
        moe = i % 2 == 1
        cw = jnp.pad(conv_w[i], ((0, HALO_ROWS - CONV_KERNEL), (0, 0)))
        rw = jnp.pad(router_w[i // 2], ((0, 0), (0, LANES - n_experts))).astype(BF16) if moe else None
        x1, x1r, route = _mix(xf, h, o, sgc, sga, cw, row2(conv_b[i]), row2(conv_ln_g[i]), row2(conv_ln_b[i]),
                              w_conv_proj[i].astype(BF16), w_attn_proj[i].astype(BF16), w_out[i].astype(BF16),
                              row2(ln1_g[i]), row2(ln1_b[i]), rw,
                              s=s, tm=tm, alpha=alpha, n_experts=n_experts if moe else 0)
        tail = (w_ple_gate[i].astype(BF16), w_ple_proj[i].astype(BF16), row2(ln2_g[i]), row2(ln2_b[i]))
        p_i = p[i].reshape(t, -1)
        if moe:
            tile_expert, tile_valid, pos, n_tiles = _moe_plan(route, n_experts, tm)
            xs = _moe_dispatch(x1r, pos, n_tiles * tm, tm=tm)
            yr = _moe_ffn(xs, tile_expert, tile_valid, exp_w_gate[i // 2].astype(BF16),
                          exp_w_up[i // 2].astype(BF16), exp_w_down[i // 2].astype(BF16), tm=tm, fc=fc)
            xf = _moe_combine(x1, p_i, route, yr, pos, *tail, tm=tm_combine, alpha=alpha)
        else:
            xf = _ffn(x1, p_i, ffn_w_gate[i // 2].astype(BF16), ffn_w_up[i // 2].astype(BF16),
                      ffn_w_down[i // 2].astype(BF16), *tail, tm=tm, fc=fc, alpha=alpha)
    return xf.reshape(bsz, s, d)
```

```python
import functools
import math

import numpy as np
import jax
import jax.numpy as jnp
from jax import lax
from jax.experimental import pallas as pl
from jax.experimental.pallas import tpu as pltpu

N_HEADS = 8
HEAD_DIM = 64
CONV_KERNEL = 31
MOBA_BLOCK = 256
MOBA_TOPK = 3
REL_BUCKETS = 32
REL_MAX_DIST = 128
TOP_K = 2
LN_EPS = 1e-5

LANES = 128
SUBLANES = 8
BF16_SUBLANES = 16
HALO_ROWS = 32
VMEM_LIMIT = 56 * 1024 * 1024

F32 = jnp.float32
BF16 = jnp.bfloat16
NEG_INF = float("-inf")


def _sigmoid(t):
    return 1.0 / (1.0 + jnp.exp(-t))


def _layer_norm(t, g, b):
    mu = jnp.mean(t, axis=-1, keepdims=True)
    d = t - mu
    var = jnp.mean(d * d, axis=-1, keepdims=True)
    return d * lax.rsqrt(var + LN_EPS) * g + b


def _dot(a, b):
    return jnp.dot(a, b, preferred_element_type=F32)


def _params(*semantics):
    return pltpu.CompilerParams(dimension_semantics=semantics, vmem_limit_bytes=VMEM_LIMIT)


def _rel_bucket(dist):
    n = jnp.maximum(dist, 0)
    max_exact = REL_BUCKETS // 2
    nf = jnp.maximum(n, 1).astype(F32)
    large = max_exact + (jnp.log(nf / max_exact) / math.log(REL_MAX_DIST / max_exact)
                         * (REL_BUCKETS - max_exact)).astype(jnp.int32)
    large = jnp.minimum(large, REL_BUCKETS - 1)
    return jnp.where(n < max_exact, n, large)


def _in_proj_kernel(x_ref, w_ref, h_ref, q_ref, k_ref, v_ref, sgc_ref, sga_ref, *, c, a, d):
    xb = x_ref[...].astype(BF16)
    o = 0
    glu_in = _dot(xb, w_ref[:, o:o + c])
    glu_gate = _dot(xb, w_ref[:, o + c:o + 2 * c])
    h_ref[...] = (glu_in * _sigmoid(glu_gate)).astype(BF16)
    o += 2 * c
    q_ref[...] = _dot(xb, w_ref[:, o:o + a]).astype(BF16)
    o += a
    k_ref[...] = _dot(xb, w_ref[:, o:o + a]).astype(BF16)
    o += a
    v_ref[...] = _dot(xb, w_ref[:, o:o + a]).astype(BF16)
    o += a
    sgc_ref[...] = _sigmoid(_dot(xb, w_ref[:, o:o + d])).astype(BF16)
    o += d
    sga_ref[...] = _sigmoid(_dot(xb, w_ref[:, o:o + d])).astype(BF16)


def _in_proj(x, w, *, c, a, tm):
    t, d = x.shape
    n = w.shape[1]
    row = lambda width: pl.BlockSpec((tm, width), lambda i: (i, 0))
    return pl.pallas_call(
        functools.partial(_in_proj_kernel, c=c, a=a, d=d),
        grid=(t // tm,),
        in_specs=[row(d), pl.BlockSpec((d, n), lambda i: (0, 0))],
        out_specs=[row(c), row(a), row(a), row(a), row(d), row(d)],
        out_shape=[jax.ShapeDtypeStruct((t, c), BF16)] + [jax.ShapeDtypeStruct((t, a), BF16)] * 3
        + [jax.ShapeDtypeStruct((t, d), BF16)] * 2,
        compiler_params=_params("arbitrary"),
        name="in_proj",
    )(x, w)


LOG2E = math.log2(math.e)
MASKED = float(jnp.finfo(jnp.bfloat16).min)
VT_ROWS = HEAD_DIM + BF16_SUBLANES


def _attn_kernel(q_ref, k_ref, v_ref, bias_ref, o_ref, vt_scr, kaug_scr, kmean_scr, sel_scr, sa_scr, sb_scr,
                 *, nb, nbp):
    blk = MOBA_BLOCK
    a = pl.program_id(2)

    @pl.when(a == 0)
    def _():
        if nbp > nb:
            kmean_scr[...] = jnp.zeros_like(kmean_scr)
        col = lax.broadcasted_iota(jnp.int32, (blk, LANES), 1)
        ones = jnp.ones((BF16_SUBLANES, blk), BF16)
        for j in range(nb):
            kb = k_ref[j * blk:(j + 1) * blk, :]
            kmean_scr[j:j + 1, :] = jnp.sum(kb.astype(F32), axis=0, keepdims=True) * (1.0 / blk)
            kaug_scr[j * blk:(j + 1) * blk, 0:LANES] = kb
            kaug_scr[j * blk:(j + 1) * blk, LANES:2 * LANES] = jnp.where(col == j, 1.0, 0.0).astype(BF16)
            vt = v_ref[j * blk:(j + 1) * blk, :].astype(F32).T.astype(BF16)
            for hh in range(2):
                vt_scr[j, hh, 0:HEAD_DIM, :] = vt[hh * HEAD_DIM:(hh + 1) * HEAD_DIM, :]
                vt_scr[j, hh, HEAD_DIM:VT_ROWS, :] = ones

    kmean = kmean_scr[...].astype(BF16)
    brow = lax.broadcasted_iota(jnp.int32, (nbp, blk), 0)
    pad_rows = jnp.zeros((LANES - nbp, blk), BF16)
    qrow = lax.broadcasted_iota(jnp.int32, (LANES, blk), 0)

    chains = [(tile, hh) for tile in range(2) for hh in range(2)]
    own = [2 * a, 2 * a + 1]
    prev = [jnp.maximum(2 * a - 1, 0), 2 * a]
    n_far = [2 * a - 1, 2 * a]
    qhs, qaugs = [], []
    for tile in range(2):
        qt = q_ref[tile * blk:(tile + 1) * blk, :].astype(F32).T * (HEAD_DIM ** -0.5 * LOG2E)
        for hh in range(2):
            lo = hh * HEAD_DIM
            qh = jnp.where((qrow >= lo) & (qrow < lo + HEAD_DIM), qt, 0.0).astype(BF16)
            qhs.append(qh)
            score = _dot(kmean, qh)
            cur = jnp.where(brow < own[tile], score, NEG_INF)
            sel = jnp.zeros((nbp, blk), dtype=jnp.bool_)
            for t in range(MOBA_TOPK):
                mx = jnp.max(cur, axis=0, keepdims=True)
                first = jnp.min(jnp.where(cur == mx, brow, nbp), axis=0, keepdims=True)
                pick = brow == first
                sel = sel | (pick & (t < own[tile]))
                cur = jnp.where(pick, NEG_INF, cur)
            sel_scr[2 * tile + hh] = jnp.where(sel, 0.0, NEG_INF)
            far_rows = jnp.where(sel & (brow < n_far[tile]), 0.0, MASKED).astype(BF16)
            qaugs.append(jnp.concatenate([qh, far_rows, pad_rows], axis=0))

    def k_block(j):
        return k_ref[pl.ds(pl.multiple_of(j * blk, blk), blk), :]

    def sel_row(j, c):
        return sel_scr[c, pl.ds(j, 1), :]

    n_units = a
    u_last = nb // 2 - 1
    nc = len(chains)

    def colmax(s0, s1):
        return jnp.maximum(jnp.max(s0, axis=0, keepdims=True), jnp.max(s1, axis=0, keepdims=True))

    def stage_ab_near(s_buf):
        cms = []
        for c, (tile, hh) in enumerate(chains):
            s0 = _dot(k_block(prev[tile]), qhs[c]) + bias_ref[0, hh, 1] + sel_row(prev[tile], c)
            s1 = _dot(k_block(own[tile]), qhs[c]) + bias_ref[0, hh, 0]
            s_buf[c, 0:blk, :] = s0
            s_buf[c, blk:2 * blk, :] = s1
            cms.append(colmax(s0, s1))
        return cms

    def stage_ab(u, s_buf):
        u = jnp.minimum(u, u_last)
        kaug = kaug_scr[pl.ds(pl.multiple_of(u * (2 * blk), 2 * blk), 2 * blk), :]
        cms = []
        for c in range(nc):
            s0 = _dot(kaug[0:blk], qaugs[c])
            s1 = _dot(kaug[blk:2 * blk], qaugs[c])
            s_buf[c, 0:blk, :] = s0
            s_buf[c, blk:2 * blk, :] = s1
            cms.append(colmax(s0, s1))
        return cms

    def stage_c(blocks, s_buf, cms, carry):
        out = []
        for c, (tile, hh) in enumerate(chains):
            j0, j1 = blocks(tile)
            m_old, acc_old = carry[2 * c:2 * c + 2]
            m_new = jnp.maximum(m_old, cms[c])
            p0 = jnp.exp2(s_buf[c, 0:blk, :] - m_new).astype(BF16)
            p1 = jnp.exp2(s_buf[c, blk:2 * blk, :] - m_new).astype(BF16)
            acc_new = (jnp.exp2(m_old - m_new) * acc_old + _dot(vt_scr[j0, hh], p0) + _dot(vt_scr[j1, hh], p1))
            out += [m_new, acc_new]
        return out

    def stage_c_far(u, s_buf, cms, carry):
        u = jnp.minimum(u, u_last)
        return stage_c(lambda tile: (2 * u, 2 * u + 1), s_buf, cms, carry)

    cm_near = stage_ab_near(sa_scr)
    cm_b = stage_ab(0, sb_scr)
    carry = []
    for c in range(nc):
        carry += [jnp.full((1, blk), NEG_INF, F32), jnp.zeros((VT_ROWS, blk), F32)]
    carry = stage_c(lambda tile: (prev[tile], own[tile]), sa_scr, cm_near, carry)

    def far_body(it, state):
        cm_b, carry = list(state[0:nc]), list(state[nc:])
        u0 = 2 * it
        cm_a = stage_ab(u0 + 1, sa_scr)
        carry = stage_c_far(u0, sb_scr, cm_b, carry)
        cm_b = stage_ab(u0 + 2, sb_scr)
        carry = stage_c_far(u0 + 1, sa_scr, cm_a, carry)
        return tuple(cm_b + carry)

    state = lax.fori_loop(0, n_units // 2, far_body, tuple(cm_b + carry))
    carry = lax.cond(n_units % 2 == 1,
                     lambda st: tuple(stage_c_far(n_units - 1, sb_scr, list(st[0:nc]), list(st[nc:]))),
                     lambda st: tuple(st[nc:]), state)
    for tile in range(2):
        outs = []
        for hh in range(2):
            acc = carry[2 * (2 * tile + hh) + 1]
            outs.append(acc[0:HEAD_DIM] / acc[HEAD_DIM:HEAD_DIM + 1])
        o_ref[tile * blk:(tile + 1) * blk, :] = jnp.concatenate(outs, axis=0).T.astype(BF16)


def _attention(q, k, v, bias_tables, *, bsz, s):
    t, a = q.shape
    blk = MOBA_BLOCK
    nb = s // blk
    nbp = -(-nb // BF16_SUBLANES) * BF16_SUBLANES
    assert nbp <= LANES and nb % 2 == 0
    npairs = a // LANES
    steps = nb // 2
    return pl.pallas_call(
        functools.partial(_attn_kernel, nb=nb, nbp=nbp),
        grid=(bsz, npairs, steps),
        in_specs=[
            pl.BlockSpec((2 * blk, LANES), lambda b, p, i: (b * steps + i, p)),
            pl.BlockSpec((s, LANES), lambda b, p, i: (b, p)),
            pl.BlockSpec((s, LANES), lambda b, p, i: (b, p)),
            pl.BlockSpec((1, 2, 2, blk, blk), lambda b, p, i: (p, 0, 0, 0, 0)),
        ],
        out_specs=pl.BlockSpec((2 * blk, LANES), lambda b, p, i: (b * steps + i, p)),
        out_shape=jax.ShapeDtypeStruct((t, a), BF16),
        scratch_shapes=[
            pltpu.VMEM((nb, 2, VT_ROWS, blk), BF16),
            pltpu.VMEM((nb * blk, 2 * LANES), BF16),
            pltpu.VMEM((nbp, LANES), F32),
            pltpu.VMEM((4, nbp, blk), F32),
            pltpu.VMEM((4, 2 * blk, blk), F32),
            pltpu.VMEM((4, 2 * blk, blk), F32),
        ],
        compiler_params=_params("arbitrary", "arbitrary", "arbitrary"),
        name="moba_attention",
    )(q, k, v, bias_tables)


def _bias_tables(rel_bias):
    blk = MOBA_BLOCK
    n = np.float32(blk + 1)
    half = REL_BUCKETS // 2
    far_bucket = half + int(np.log(n / np.float32(half)) / math.log(REL_MAX_DIST / half) * (REL_BUCKETS - half))
    assert far_bucket >= REL_BUCKETS - 1
    bias_t = rel_bias.T.astype(F32)
    bias_t = (bias_t - bias_t[:, REL_BUCKETS - 1:]) * LOG2E
    kk = jnp.arange(blk, dtype=jnp.int32)[:, None]
    qq = jnp.arange(blk, dtype=jnp.int32)[None, :]

    def lookup(bucket):
        out = jnp.zeros((bias_t.shape[0],) + bucket.shape, F32)
        for b in range(REL_BUCKETS):
            out = jnp.where(bucket[None] == b, bias_t[:, b][:, None, None], out)
        return out

    d_own = qq - kk
    own = jnp.where(d_own >= 0, lookup(_rel_bucket(d_own)), NEG_INF)
    prev = lookup(_rel_bucket(qq + blk - kk))
    tables = jnp.stack([own, prev], axis=1)
    return tables.reshape(N_HEADS // 2, 2, 2, blk, blk)


def _mix_kernel(x_ref, h_ref, halo_ref, o_ref, sgc_ref, sga_ref, cw_ref, cb_ref, cg_ref, cbeta_ref,
                wcp_ref, wap_ref, wout_ref, g1_ref, b1_ref, *rest, tm, tiles_per_seq, alpha, n_experts):
    if n_experts:
        rw_ref, x1_ref, x1r_ref, route_ref, hc_scr, sh_scr = rest
    else:
        x1_ref, hc_scr, sh_scr = rest
    i = pl.program_id(0)
    first = (i % tiles_per_seq) == 0
    hc_scr[0:HALO_ROWS, :] = jnp.where(first, 0.0, halo_ref[...].astype(F32))
    hc_scr[HALO_ROWS:, :] = h_ref[...].astype(F32)
    span = sh_scr.shape[1]
    for ph in range(1, SUBLANES):
        sh_scr[ph - 1] = hc_scr[ph:ph + span, :]

    def tap(kk):
        off = HALO_ROWS - (CONV_KERNEL - 1) + kk
        ph, start = off % SUBLANES, off - off % SUBLANES
        rows = hc_scr[start:start + tm, :] if ph == 0 else sh_scr[ph - 1, start:start + tm, :]
        return cw_ref[kk:kk + 1, :] * rows

    acc = cb_ref[...] + tap(0)
    for kk in range(1, CONV_KERNEL):
        acc = acc + tap(kk)
    hn = _layer_norm(acc, cg_ref[...], cbeta_ref[...])
    hs = (hn * _sigmoid(hn)).astype(BF16)
    y_conv = _dot(hs, wcp_ref[...])
    y_attn = _dot(o_ref[...], wap_ref[...])
    mixed = sgc_ref[...].astype(F32) * y_conv + sga_ref[...].astype(F32) * y_attn
    z = alpha * x_ref[...] + _dot(mixed.astype(BF16), wout_ref[...])
    x1 = _layer_norm(z, g1_ref[...], b1_ref[...])
    x1_ref[...] = x1
    if n_experts:
        nchunk = x1.shape[1] // LANES
        for cc in range(nchunk):
            x1r_ref[pl.ds(cc, tm, stride=nchunk), :] = x1[:, cc * LANES:(cc + 1) * LANES]
        logits = _dot(x1.astype(BF16), rw_ref[...])
        lane = lax.broadcasted_iota(jnp.int32, logits.shape, 1)
        lg = jnp.where(lane < n_experts, logits, NEG_INF)
        v1 = jnp.max(lg, axis=-1, keepdims=True)
        i1 = jnp.min(jnp.where(lg == v1, lane, LANES), axis=-1, keepdims=True)
        lg2 = jnp.where(lane == i1, NEG_INF, lg)
        v2 = jnp.max(lg2, axis=-1, keepdims=True)
        i2 = jnp.min(jnp.where(lg2 == v2, lane, LANES), axis=-1, keepdims=True)
        e = jnp.exp(v2 - v1)
        route_ref[...] = (jnp.where(lane == 0, i1.astype(F32), 0.0) + jnp.where(lane == 1, i2.astype(F32), 0.0)
                          + jnp.where(lane == 2, 1.0 / (1.0 + e), 0.0) + jnp.where(lane == 3, e / (1.0 + e), 0.0))


def _mix(x, h, o, sgc, sga, cw, cb, cg, cbeta, wcp, wap, wout, g1, b1, rw, *, s, tm, alpha, n_experts):
    t, d = x.shape
    c = h.shape[1]
    a = o.shape[1]
    row = lambda width: pl.BlockSpec((tm, width), lambda i: (i, 0))
    full = lambda arr: pl.BlockSpec(arr.shape, lambda i: (0,) * arr.ndim)
    halo_blocks = tm // HALO_ROWS
    in_specs = [row(d), row(c),
                pl.BlockSpec((HALO_ROWS, c), lambda i: (jnp.maximum(i * halo_blocks - 1, 0), 0)),
                row(a), row(d), row(d),
                full(cw), full(cb), full(cg), full(cbeta), full(wcp), full(wap), full(wout), full(g1), full(b1)]
    args = [x, h, h, o, sgc, sga, cw, cb, cg, cbeta, wcp, wap, wout, g1, b1]
    out_specs = [row(d)]
    out_shape = [jax.ShapeDtypeStruct((t, d), F32)]
    if n_experts:
        assert d == SUBLANES * LANES
        in_specs.append(full(rw))
        args.append(rw)
        out_specs += [pl.BlockSpec((tm * SUBLANES, LANES), lambda i: (i, 0)), row(LANES)]
        out_shape += [jax.ShapeDtypeStruct((t * SUBLANES, LANES), F32), jax.ShapeDtypeStruct((t, LANES), F32)]
    res = pl.pallas_call(
        functools.partial(_mix_kernel, tm=tm, tiles_per_seq=s // tm, alpha=alpha, n_experts=n_experts),
        grid=(t // tm,),
        in_specs=in_specs,
        out_specs=out_specs,
        out_shape=out_shape,
        scratch_shapes=[pltpu.VMEM((tm + HALO_ROWS, c), F32),
                        pltpu.VMEM((SUBLANES - 1, tm + HALO_ROWS - SUBLANES, c), F32)],
        compiler_params=_params("arbitrary"),
        name="mix",
    )(*args)
    return res if n_experts else (res[0], None, None)


def _ple(xb, p_ref, wpg_ref, wpp_ref):
    return _sigmoid(_dot(xb, wpg_ref[...])) * _dot(p_ref[...].astype(BF16), wpp_ref[...])


def _ffn_kernel(x1_ref, p_ref, wg_ref, wu_ref, wd_ref, wpg_ref, wpp_ref, g2_ref, b2_ref, out_ref,
                xb_scr, acc_scr, *, alpha):
    j = pl.program_id(1)

    @pl.when(j == 0)
    def _():
        xb_scr[...] = x1_ref[...].astype(BF16)
        acc_scr[...] = jnp.zeros_like(acc_scr)

    xb = xb_scr[...]
    g = _dot(xb, wg_ref[...])
    u = _dot(xb, wu_ref[...])
    acc_scr[...] += _dot((g * _sigmoid(g) * u).astype(BF16), wd_ref[...])

    @pl.when(j == pl.num_programs(1) - 1)
    def _():
        ple = _ple(xb, p_ref, wpg_ref, wpp_ref)
        out_ref[...] = _layer_norm(alpha * x1_ref[...] + acc_scr[...] + ple, g2_ref[...], b2_ref[...])


def _ffn(x1, p, wg, wu, wd, wpg, wpp, g2, b2, *, tm, fc, alpha):
    t, d = x1.shape
    f = wg.shape[1]
    row = lambda width: pl.BlockSpec((tm, width), lambda i, j: (i, 0))
    full = lambda arr: pl.BlockSpec(arr.shape, lambda i, j: (0,) * arr.ndim)
    return pl.pallas_call(
        functools.partial(_ffn_kernel, alpha=alpha),
        grid=(t // tm, f // fc),
        in_specs=[row(d), row(p.shape[1]),
                  pl.BlockSpec((d, fc), lambda i, j: (0, j)),
                  pl.BlockSpec((d, fc), lambda i, j: (0, j)),
                  pl.BlockSpec((fc, d), lambda i, j: (j, 0)),
                  full(wpg), full(wpp), full(g2), full(b2)],
        out_specs=row(d),
        out_shape=jax.ShapeDtypeStruct((t, d), F32),
        scratch_shapes=[pltpu.VMEM((tm, d), BF16), pltpu.VMEM((tm, d), F32)],
        compiler_params=_params("arbitrary", "arbitrary"),
        name="ffn",
    )(x1, p, wg, wu, wd, wpg, wpp, g2, b2)


DMA_UNROLL = 8
N_DMA_PRIORITIES = 2


def _moe_plan(route, n_experts, tm):
    e_slot = route[:, 0:TOP_K].astype(jnp.int32).reshape(-1)
    n_slots = e_slot.shape[0]
    ids = jnp.arange(n_experts, dtype=jnp.int32)
    onehot = (e_slot[:, None] == ids[None, :]).astype(jnp.int32)
    csum = jnp.cumsum(onehot, axis=0)
    cnt = csum[-1]
    padded = ((cnt + tm - 1) // tm) * tm
    off_end = jnp.cumsum(padded)
    off = off_end - padded
    pos = jnp.sum((csum - onehot + off[None, :]) * onehot, axis=1)
    n_tiles = n_slots // tm + n_experts
    tile_start = jnp.arange(n_tiles, dtype=jnp.int32) * tm
    tile_expert = jnp.minimum(jnp.sum((tile_start[:, None] >= off_end[None, :]).astype(jnp.int32), axis=1),
                              n_experts - 1)
    tile_valid = (tile_start < off_end[-1]).astype(jnp.int32)
    return tile_expert, tile_valid, pos, n_tiles


def _row_copy(src, src_row, dst, dst_row, sem):
    return pltpu.make_async_copy(
        src.at[pl.ds(pl.multiple_of(src_row * SUBLANES, SUBLANES), SUBLANES), :],
        dst.at[pl.ds(pl.multiple_of(dst_row * SUBLANES, SUBLANES), SUBLANES), :], sem)


def _moe_dispatch_kernel(pos_ref, x1r_ref, init_ref, xs_ref, sem, *, tm):
    del init_ref

    def copy(tok, kk):
        return _row_copy(x1r_ref, tok, xs_ref, pos_ref[0, 0, TOP_K * tok + kk], sem)

    def issue(tok, carry):
        for kk in range(TOP_K):
            copy(tok, kk).start(priority=kk % N_DMA_PRIORITIES)
        return carry

    lax.fori_loop(0, tm, issue, 0, unroll=DMA_UNROLL)

    def drain(tok, carry):
        for kk in range(TOP_K):
            copy(tok, kk).wait()
        return carry

    lax.fori_loop(0, tm, drain, 0, unroll=DMA_UNROLL)


def _moe_dispatch(x1r, pos, n_rows, *, tm):
    t = x1r.shape[0] // SUBLANES
    pos3 = pos.reshape(t // tm, 1, TOP_K * tm)
    init = jnp.zeros((n_rows * SUBLANES, LANES), F32)
    return pl.pallas_call(
        functools.partial(_moe_dispatch_kernel, tm=tm),
        grid=(t // tm,),
        in_specs=[pl.BlockSpec((1, 1, TOP_K * tm), lambda i: (i, 0, 0), memory_space=pltpu.SMEM),
                  pl.BlockSpec((tm * SUBLANES, LANES), lambda i: (i, 0)), pl.BlockSpec(memory_space=pl.ANY)],
        out_specs=pl.BlockSpec(memory_space=pl.ANY),
        out_shape=jax.ShapeDtypeStruct(init.shape, F32),
        scratch_shapes=[pltpu.SemaphoreType.DMA],
        input_output_aliases={2: 0},
        compiler_params=_params("arbitrary"),
        name="moe_dispatch",
    )(pos3, x1r, init)


def _moe_ffn_kernel(te_ref, valid_ref, xs_ref, wg_ref, wu_ref, wd_ref, y_ref, xb_scr, acc_scr, *, tm):
    i = pl.program_id(0)
    j = pl.program_id(1)
    valid = valid_ref[i] != 0
    nchunk = xb_scr.shape[1] // LANES

    @pl.when(valid & (j == 0))
    def _():
        for cc in range(nchunk):
            xb_scr[:, cc * LANES:(cc + 1) * LANES] = xs_ref[pl.ds(cc, tm, stride=nchunk), :].astype(BF16)
        acc_scr[...] = jnp.zeros_like(acc_scr)

    @pl.when(valid)
    def _():
        xb = xb_scr[...]
        g = _dot(xb, wg_ref[0])
        u = _dot(xb, wu_ref[0])
        acc_scr[...] += _dot((g * _sigmoid(g) * u).astype(BF16), wd_ref[0])

    last = j == pl.num_programs(1) - 1

    @pl.when(last & valid)
    def _():
        for cc in range(nchunk):
            y_ref[pl.ds(cc, tm, stride=nchunk), :] = acc_scr[:, cc * LANES:(cc + 1) * LANES]

    @pl.when(last & jnp.logical_not(valid))
    def _():
        y_ref[...] = jnp.zeros_like(y_ref)


def _moe_ffn(xs, tile_expert, tile_valid, wg, wu, wd, *, tm, fc):
    ne, d, f = wg.shape
    n_tiles = xs.shape[0] // (tm * SUBLANES)
    nj = f // fc
    jj = lambda i, j, te, tv: jnp.where(tv[i] != 0, j, nj - 1)
    rows = pl.BlockSpec((tm * SUBLANES, LANES), lambda i, j, te, tv: (i, 0))
    return pl.pallas_call(
        functools.partial(_moe_ffn_kernel, tm=tm),
        grid_spec=pltpu.PrefetchScalarGridSpec(
            num_scalar_prefetch=2,
            grid=(n_tiles, nj),
            in_specs=[
                rows,
                pl.BlockSpec((1, d, fc), lambda i, j, te, tv: (te[i], 0, jj(i, j, te, tv))),
                pl.BlockSpec((1, d, fc), lambda i, j, te, tv: (te[i], 0, jj(i, j, te, tv))),
                pl.BlockSpec((1, fc, d), lambda i, j, te, tv: (te[i], jj(i, j, te, tv), 0)),
            ],
            out_specs=rows,
            scratch_shapes=[pltpu.VMEM((tm, d), BF16), pltpu.VMEM((tm, d), F32)],
        ),
        out_shape=jax.ShapeDtypeStruct(xs.shape, F32),
        compiler_params=_params("arbitrary", "arbitrary"),
        name="moe_ffn",
    )(tile_expert, tile_valid, xs, wg, wu, wd)


def _moe_combine_kernel(pos_ref, x1_ref, p_ref, route_ref, yr_ref, wpg_ref, wpp_ref, g2_ref, b2_ref, out_ref,
                        yg_scr, sem, *, tm, alpha):
    d = x1_ref.shape[1]
    nchunk = d // LANES

    def copy(tok, kk):
        return _row_copy(yr_ref, pos_ref[0, 0, TOP_K * tok + kk], yg_scr, kk * tm + tok, sem)

    def issue(tok, carry):
        for kk in range(TOP_K):
            copy(tok, kk).start(priority=kk % N_DMA_PRIORITIES)
        return carry

    lax.fori_loop(0, tm, issue, 0, unroll=DMA_UNROLL)
    x1 = x1_ref[...]
    ple = _ple(x1.astype(BF16), p_ref, wpg_ref, wpp_ref)

    def drain(tok, carry):
        for kk in range(TOP_K):
            copy(tok, kk).wait()
        return carry

    lax.fori_loop(0, tm, drain, 0, unroll=DMA_UNROLL)
    route = route_ref[...]
    gates = [route[:, TOP_K + kk:TOP_K + kk + 1] for kk in range(TOP_K)]
    chunks = []
    for cc in range(nchunk):
        f = gates[0] * yg_scr[pl.ds(cc, tm, stride=nchunk), :]
        for kk in range(1, TOP_K):
            f = f + gates[kk] * yg_scr[pl.ds(kk * tm * nchunk + cc, tm, stride=nchunk), :]
        chunks.append(f)
    ffn = jnp.concatenate(chunks, axis=1)
    out_ref[...] = _layer_norm(alpha * x1 + ffn + ple, g2_ref[...], b2_ref[...])


def _moe_combine(x1, p, route, yr, pos, wpg, wpp, g2, b2, *, tm, alpha):
    t, d = x1.shape
    row = lambda width: pl.BlockSpec((tm, width), lambda i: (i, 0))
    full = lambda arr: pl.BlockSpec(arr.shape, lambda i: (0,) * arr.ndim)
    pos3 = pos.reshape(t // tm, 1, TOP_K * tm)
    return pl.pallas_call(
        functools.partial(_moe_combine_kernel, tm=tm, alpha=alpha),
        grid=(t // tm,),
        in_specs=[pl.BlockSpec((1, 1, TOP_K * tm), lambda i: (i, 0, 0), memory_space=pltpu.SMEM),
                  row(d), row(p.shape[1]), row(LANES), pl.BlockSpec(memory_space=pl.ANY),
                  full(wpg), full(wpp), full(g2), full(b2)],
        out_specs=row(d),
        out_shape=jax.ShapeDtypeStruct((t, d), F32),
        scratch_shapes=[pltpu.VMEM((TOP_K * tm * SUBLANES, LANES), F32), pltpu.SemaphoreType.DMA],
        compiler_params=_params("arbitrary"),
        name="moe_combine",
    )(pos3, x1, p, route, yr, wpg, wpp, g2, b2)


def kernel(x, p, rel_bias, w_in, conv_w, conv_b, conv_ln_g, conv_ln_b, w_conv_proj, w_attn_proj, w_out, ln1_g, ln1_b, w_ple_gate, w_ple_proj, ln2_g, ln2_b, ffn_w_gate, ffn_w_up, ffn_w_down, router_w, exp_w_gate, exp_w_up, exp_w_down):
    bsz, s, d = x.shape
    depth = w_in.shape[0]
    c = conv_w.shape[2]
    a = w_attn_proj.shape[1]
    n_experts = router_w.shape[2]
    assert a == N_HEADS * HEAD_DIM and s % (2 * MOBA_BLOCK) == 0 and conv_w.shape[1] == CONV_KERNEL
    assert n_experts >= TOP_K
    alpha = (2 * depth) ** 0.25
    t = bsz * s
    tm = 512
    tm_combine = 256
    fc = 1792
    assert s % tm == 0 and (TOP_K * t) % tm == 0

    tables = _bias_tables(rel_bias)
    xf = x.reshape(t, d)
    row2 = lambda v: v.reshape(1, -1).astype(F32)
    for i in range(depth):
        h, q, k, v, sgc, sga = _in_proj(xf, w_in[i].astype(BF16), c=c, a=a, tm=tm)
        o = _attention(q, k, v, tables, bsz=bsz, s=s)
        moe = i % 2 == 1
        cw = jnp.pad(conv_w[i], ((0, HALO_ROWS - CONV_KERNEL), (0, 0)))
        rw = jnp.pad(router_w[i // 2], ((0, 0), (0, LANES - n_experts))).astype(BF16) if moe else None
        x1, x1r, route = _mix(xf, h, o, sgc, sga, cw, row2(conv_b[i]), row2(conv_ln_g[i]), row2(conv_ln_b[i]),
                              w_conv_proj[i].astype(BF16), w_attn_proj[i].astype(BF16), w_out[i].astype(BF16),
                              row2(ln1_g[i]), row2(ln1_b[i]), rw,
                              s=s, tm=tm, alpha=alpha, n_experts=n_experts if moe else 0)
        tail = (w_ple_gate[i].astype(BF16), w_ple_proj[i].astype(BF16), row2(ln2_g[i]), row2(ln2_b[i]))
        p_i = p[i].reshape(t, -1)
        if moe:
            tile_expert, tile_valid, pos, n_tiles = _moe_plan(route, n_experts, tm)
            xs = _moe_dispatch(x1r, pos, n_tiles * tm, tm=tm)
            yr = _moe_ffn(xs, tile_expert, tile_valid, exp_w_gate[i // 2].astype(BF16),
                          exp_w_up[i // 2].astype(BF16), exp_w_down[i // 2].astype(BF16), tm=tm, fc=fc)
            xf = _moe_combine(x1, p_i, route, yr, pos, *tail, tm=tm_combine, alpha=alpha)
        else:
            xf = _ffn(x1, p_i, ffn_w_gate[i // 2].astype(BF16), ffn_w_up[i // 2].astype(BF16),
                      ffn_w_down[i // 2].astype(BF16), *tail, tm=tm, fc=fc, alpha=alpha)
    return xf.reshape(bsz, s, d)
```
